```python
import jax, jax.numpy as jnp
from jax import lax
import numpy as np

D_MODEL = 4096
BATCH = 1
SEQ = 8192
DEPTH = 4

CHUNK = 64
P_DIM = 256
D_FF = 5 * D_MODEL // 8
MIX_WIDTH = D_MODEL
HG_WIDTH = 3 * MIX_WIDTH // 8
HG_DK = 128
HG_HEADS = HG_WIDTH // HG_DK
HG_DV = HG_WIDTH // HG_HEADS
SSM_WIDTH = 3 * MIX_WIDTH // 8
SSM_HEADDIM = 64
SSM_HEADS = SSM_WIDTH // SSM_HEADDIM
SSM_GROUPS = 8
SSM_HPG = SSM_HEADS // SSM_GROUPS
SSM_STATE = 128
SSM_CONV = 4
SSM_XBC = SSM_WIDTH + 2 * SSM_GROUPS * SSM_STATE
GLA_WIDTH = MIX_WIDTH // 4
GLA_HEADS = 4
GLA_DV = GLA_WIDTH // GLA_HEADS
GLA_DK = GLA_DV // 2
GLA_RANK = 16
GLA_TAU = 16.0
N_BRANCH = 3
BRANCH_WIDTHS = (HG_WIDTH, SSM_WIDTH, GLA_WIDTH)
BRANCH_OFFSETS = (HG_WIDTH, HG_WIDTH + SSM_WIDTH)
NORM_GROUPS = (HG_HEADS, SSM_GROUPS, GLA_HEADS)
IN_SPLITS = (HG_HEADS * HG_DK, HG_HEADS * HG_DK, HG_WIDTH, HG_WIDTH,
             SSM_WIDTH, SSM_XBC, SSM_HEADS,
             GLA_HEADS * GLA_DK, GLA_HEADS * GLA_DK, GLA_WIDTH, GLA_WIDTH,
             GLA_RANK,
             N_BRANCH * D_MODEL)
IN_COLS = sum(IN_SPLITS)
IN_OFFSETS = tuple(int(o) for o in np.cumsum(IN_SPLITS)[:-1])
ALPHA = float((2 * DEPTH) ** 0.25)
BETA = float((8 * DEPTH) ** -0.25)

kernel_name = "hybrid_hgrn2_ssd_gla_macaron_deepnorm"


def layer_norm(x, g, b, eps=1e-5):
    xf = x.astype(jnp.float32)
    mu = jnp.mean(xf, -1, keepdims=True)
    var = jnp.mean(jnp.square(xf - mu), -1, keepdims=True)
    return ((xf - mu) * lax.rsqrt(var + eps) * g + b).astype(x.dtype)


def rms_norm_groups(x, g, n_groups, eps=1e-6):
    shp = x.shape
    xf = x.astype(jnp.float32).reshape(shp[:-1] + (n_groups, shp[-1] // n_groups))
    xf = xf * lax.rsqrt(jnp.mean(jnp.square(xf), -1, keepdims=True) + eps)
    return (xf.reshape(shp) * g).astype(x.dtype)


def swiglu(x, w_in, w_out):
    gate, up = jnp.split(x @ w_in, 2, axis=-1)
    return (jax.nn.silu(gate) * up) @ w_out


def to_chunks(a):
    bsz, t = a.shape[:2]
    return jnp.moveaxis(a.reshape((bsz, t // CHUNK, CHUNK) + a.shape[2:]), 1, 0)


def from_chunks(a):
    a = jnp.moveaxis(a, 0, 1)
    return a.reshape((a.shape[0], a.shape[1] * a.shape[2]) + a.shape[3:])


def chunked_gla(q, k, v, log_f):
    bsz, _, h, dk = q.shape
    dv = v.shape[-1]
    xs = tuple(to_chunks(a.astype(jnp.float32)) for a in (q, k, v, log_f))
    causal = jnp.tril(jnp.ones((CHUNK, CHUNK), bool))[None, :, :, None, None]

    def step(state, inp):
        qi, ki, vi, gi = inp
        b = jnp.cumsum(gi, axis=1)
        diff = b[:, :, None] - b[:, None, :]
        decay = jnp.exp(jnp.where(causal, diff, -jnp.inf))
        scores = jnp.einsum('bthd,bshd,btshd->bhts', qi, ki, decay)
        o = jnp.einsum('bhts,bshv->bthv', scores, vi)
        o = o + jnp.einsum('bthd,bhdv->bthv', qi * jnp.exp(b), state)
        b_last = b[:, -1]
        k_dec = ki * jnp.exp(b_last[:, None] - b)
        state = state * jnp.exp(b_last)[..., None] + jnp.einsum('bshd,bshv->bhdv', k_dec, vi)
        return state, o

    s0 = jnp.zeros((bsz, h, dk, dv), jnp.float32)
    _, o = lax.scan(step, s0, xs)
    return from_chunks(o).astype(v.dtype)


def chunked_ssd(x, dt, a, bmat, cmat):
    bsz = x.shape[0]
    xs = tuple(to_chunks(t_.astype(jnp.float32)) for t_ in (x, dt, bmat, cmat))
    a = a.astype(jnp.float32)
    causal = jnp.tril(jnp.ones((CHUNK, CHUNK), bool))[None, :, :, None, None]

    def step(state, inp):
        xi, dti, bi, ci = inp
        cs = jnp.cumsum(dti * a, axis=1)
        diff = cs[:, :, None] - cs[:, None, :]
        lmat = jnp.exp(jnp.where(causal, diff, -jnp.inf))
        cb = jnp.einsum('btgn,bsgn->btsg', ci, bi)
        xdt = xi * dti[..., None]
        y = jnp.einsum('btsg,btsgh,bsghp->btghp', cb, lmat, xdt)
        y = y + jnp.einsum('btgn,bghpn->btghp', ci, state) * jnp.exp(cs)[..., None]
        last = cs[:, -1]
        w = jnp.exp(last[:, None] - cs)[..., None]
        state = state * jnp.exp(last)[..., None, None] + jnp.einsum('bsghp,bsgn->bghpn', xdt * w, bi)
        return state, y

    s0 = jnp.zeros((bsz, SSM_GROUPS, SSM_HPG, SSM_HEADDIM, SSM_STATE), jnp.float32)
    _, y = lax.scan(step, s0, xs)
    return from_chunks(y)


def causal_depthwise_conv(x, w, b):
    y = lax.conv_general_dilated(x, w[:, None, :], window_strides=(1,), padding=[(SSM_CONV - 1, 0)],
                                 dimension_numbers=('NWC', 'WIO', 'NWC'), feature_group_count=x.shape[-1])
    return y + b


def hybrid_mixer(h, lb, w_in, conv_w, conv_b, dt_bias, a_log, d_skip, gla_w_gate, gla_b_gate,
                 mix_norm_g, w_branch, w_out):
    bsz, t, _ = h.shape
    heads = lambda a_, n: a_.reshape(bsz, t, n, -1)
    (hg_q, hg_f, hg_i, hg_g, ssm_z, ssm_xbc, ssm_dt, gla_q, gla_k, gla_v, gla_g, gla_r,
     gates) = jnp.split(h @ w_in, IN_OFFSETS, axis=-1)

    zf = hg_f.astype(jnp.float32)
    log_f = jnp.logaddexp(jnp.log(lb), jnp.log1p(-lb) + jax.nn.log_sigmoid(zf))
    k_hg = (1.0 - lb) * jax.nn.sigmoid(-zf)
    o_hg = chunked_gla(heads(jax.nn.silu(hg_q) * HG_DK ** -0.5, HG_HEADS), heads(k_hg, HG_HEADS),
                       heads(hg_i, HG_HEADS), heads(log_f, HG_HEADS)).reshape(bsz, t, HG_WIDTH)

    xbc = jax.nn.silu(causal_depthwise_conv(ssm_xbc, conv_w, conv_b))
    sx, sb, sc = jnp.split(xbc, (SSM_WIDTH, SSM_WIDTH + SSM_GROUPS * SSM_STATE), axis=-1)
    dt = jax.nn.softplus(ssm_dt.astype(jnp.float32) + dt_bias)
    a = -jnp.exp(a_log.astype(jnp.float32))
    x5 = sx.reshape(bsz, t, SSM_GROUPS, SSM_HPG, SSM_HEADDIM)
    y_ssm = chunked_ssd(x5, dt.reshape(bsz, t, SSM_GROUPS, SSM_HPG), a.reshape(SSM_GROUPS, SSM_HPG),
                        sb.reshape(bsz, t, SSM_GROUPS, SSM_STATE), sc.reshape(bsz, t, SSM_GROUPS, SSM_STATE))
    y_ssm = y_ssm + d_skip.reshape(SSM_GROUPS, SSM_HPG)[:, :, None] * x5
    y_ssm = y_ssm.reshape(bsz, t, SSM_WIDTH).astype(h.dtype) * jax.nn.silu(ssm_z)

    log_alpha = jax.nn.log_sigmoid((gla_r @ gla_w_gate + gla_b_gate).astype(jnp.float32)) / GLA_TAU
    o_gla = chunked_gla(heads(gla_q * GLA_DK ** -0.5, GLA_HEADS), heads(gla_k, GLA_HEADS),
                        heads(gla_v, GLA_HEADS), heads(log_alpha, GLA_HEADS)).reshape(bsz, t, GLA_WIDTH)

    g_hg, g_ssm, g_gla = jnp.split(mix_norm_g, BRANCH_OFFSETS)
    w_hg, w_ssm, w_gla = jnp.split(w_branch, BRANCH_OFFSETS, axis=0)
    y_hg = rms_norm_groups(o_hg, g_hg, HG_HEADS) * jax.nn.sigmoid(hg_g)
    y_ssm = rms_norm_groups(y_ssm, g_ssm, SSM_GROUPS)
    y_gla = rms_norm_groups(o_gla, g_gla, GLA_HEADS) * jax.nn.silu(gla_g)

    gate = jax.nn.sigmoid(gates).reshape(bsz, t, N_BRANCH, D_MODEL)
    merged = (gate[:, :, 0] * (y_hg @ w_hg) + gate[:, :, 1] * (y_ssm @ w_ssm)
              + gate[:, :, 2] * (y_gla @ w_gla))
    return merged @ w_out


def setup_inputs(seed: int = 0) -> dict:
    key = jax.random.key(seed)
    ks = jax.random.split(key, 22)
    nrm = lambda k_, shp, s: jax.random.normal(k_, shp, jnp.float32) * s
    row_scale = jnp.concatenate([jnp.full((w_,), w_ ** -0.5, jnp.float32) for w_ in BRANCH_WIDTHS])
    dt0 = jnp.exp(jax.random.uniform(ks[12], (DEPTH, SSM_HEADS), jnp.float32, np.log(1e-3), np.log(1e-1)))
    return {
        "x": nrm(ks[0], (BATCH, SEQ, D_MODEL), 1.0),
        "p": nrm(ks[1], (DEPTH, BATCH, SEQ, P_DIM), 1.0),
        "ffn_w_in": nrm(ks[2], (DEPTH, 2, D_MODEL, 2 * D_FF), D_MODEL ** -0.5),
        "ffn_w_out": nrm(ks[3], (DEPTH, 2, D_FF, D_MODEL), BETA * D_FF ** -0.5),
        "ln_g": 1.0 + nrm(ks[4], (DEPTH, 3, D_MODEL), 0.02),
        "ln_b": nrm(ks[5], (DEPTH, 3, D_MODEL), 0.02),
        "w_in": nrm(ks[6], (DEPTH, D_MODEL, IN_COLS), D_MODEL ** -0.5),
        "hg_lb_logits": nrm(ks[7], (DEPTH, HG_HEADS * HG_DK), 0.1),
        "ssm_conv_w": nrm(ks[8], (DEPTH, SSM_CONV, SSM_XBC), SSM_CONV ** -0.5),
        "ssm_conv_b": nrm(ks[9], (DEPTH, SSM_XBC), 0.02),
        "ssm_dt_bias": dt0 + jnp.log(-jnp.expm1(-dt0)),
        "ssm_a_log": jnp.log(jax.random.uniform(ks[10], (DEPTH, SSM_HEADS), jnp.float32, 1.0, 16.0)),
        "ssm_d": 1.0 + nrm(ks[11], (DEPTH, SSM_HEADS), 0.1),
        "gla_w_gate": nrm(ks[13], (DEPTH, GLA_RANK, GLA_HEADS * GLA_DK), GLA_RANK ** -0.5),
        "gla_b_gate": nrm(ks[14], (DEPTH, GLA_HEADS * GLA_DK), 0.1),
        "mix_norm_g": 1.0 + nrm(ks[15], (DEPTH, MIX_WIDTH), 0.02),
        "w_branch": nrm(ks[16], (DEPTH, MIX_WIDTH, D_MODEL), BETA) * row_scale[None, :, None],
        "w_out": nrm(ks[17], (DEPTH, D_MODEL, D_MODEL), BETA * D_MODEL ** -0.5),
        "ple_w_proj": nrm(ks[18], (DEPTH, P_DIM, D_MODEL), P_DIM ** -0.5),
        "ple_w_gate": nrm(ks[19], (DEPTH, D_MODEL, D_MODEL), D_MODEL ** -0.5),
    }


def reference(x, p, ffn_w_in, ffn_w_out, ln_g, ln_b, w_in, hg_lb_logits, ssm_conv_w, ssm_conv_b,
              ssm_dt_bias, ssm_a_log, ssm_d, gla_w_gate, gla_b_gate, mix_norm_g, w_branch, w_out,
              ple_w_proj, ple_w_gate):
    lb_all = jnp.cumsum(jax.nn.softmax(hg_lb_logits.astype(jnp.float32), axis=0), axis=0)
    lb_all = lb_all - lb_all[0]
    h = x
    for i in range(DEPTH):
        h = layer_norm(ALPHA * h + 0.5 * swiglu(h, ffn_w_in[i, 0], ffn_w_out[i, 0]), ln_g[i, 0], ln_b[i, 0])
        mix = hybrid_mixer(h, lb_all[i], w_in[i], ssm_conv_w[i], ssm_conv_b[i], ssm_dt_bias[i], ssm_a_log[i],
                           ssm_d[i], gla_w_gate[i], gla_b_gate[i], mix_norm_g[i], w_branch[i], w_out[i])
        h = layer_norm(ALPHA * h + mix, ln_g[i, 1], ln_b[i, 1])
        h = layer_norm(ALPHA * h + 0.5 * swiglu(h, ffn_w_in[i, 1], ffn_w_out[i, 1]), ln_g[i, 2], ln_b[i, 2])
        h = h + jax.nn.sigmoid(h @ ple_w_gate[i]) * (p[i] @ ple_w_proj[i])
    return h
```

```python
import functools

import jax
import jax.numpy as jnp
import numpy as np
from jax import lax
from jax.experimental import pallas as pl
from jax.experimental.pallas import tpu as pltpu

F32 = jnp.float32
BF16 = jnp.bfloat16

D_MODEL = 4096
DEPTH = 4
P_DIM = 256
D_FF = 5 * D_MODEL // 8
HG_WIDTH = 1536
HG_DK = 128
HG_HEADS = 12
SSM_WIDTH = 1536
SSM_HEADDIM = 64
SSM_HEADS = 24
SSM_GROUPS = 8
SSM_HPG = 3
SSM_STATE = 128
SSM_CONV = 4
SSM_XBC = SSM_WIDTH + 2 * SSM_GROUPS * SSM_STATE
GLA_WIDTH = 1024
GLA_HEADS = 4
GLA_DV = 256
GLA_DK = 128
GLA_RANK = 16
GLA_TAU = 16.0
IN_SPLITS = (HG_WIDTH, HG_WIDTH, HG_WIDTH, HG_WIDTH, SSM_WIDTH, SSM_XBC, SSM_HEADS,
             GLA_HEADS * GLA_DK, GLA_HEADS * GLA_DK, GLA_WIDTH, GLA_WIDTH, GLA_RANK, 3 * D_MODEL)
IN_OFFSETS = tuple(int(o) for o in np.cumsum((0,) + IN_SPLITS))
ALPHA = float((2 * DEPTH) ** 0.25)
LN_EPS = 1e-5
RMS_EPS = 1e-6

VMEM_LIMIT_BYTES = 56 * 1024 * 1024
LANES = 128
SUBLANES = 8

GLA_CHUNK = 64
SSD_CHUNK = 128
SSD_PAIR = 2


def _cparams(*sem):
    return pltpu.CompilerParams(dimension_semantics=sem, vmem_limit_bytes=VMEM_LIMIT_BYTES)


def _sigmoid(x):
    return 1.0 / (1.0 + jnp.exp(-x))


def _silu(x):
    return x * _sigmoid(x)


def _log_sigmoid(x):
    return jnp.minimum(x, 0.0) - jnp.log1p(jnp.exp(-jnp.abs(x)))


def _softplus(x):
    return jnp.maximum(x, 0.0) + jnp.log1p(jnp.exp(-jnp.abs(x)))


def _dot(a, b):
    return jnp.dot(a, b, preferred_element_type=F32)


def _dot_nt(a, b):
    return lax.dot_general(a, b, (((1,), (1,)), ((), ())), preferred_element_type=F32)


def _dot_tn(a, b):
    return lax.dot_general(a, b, (((0,), (0,)), ((), ())), preferred_element_type=F32)


def _dot_exact(a, b):
    return jnp.dot(a, b, preferred_element_type=F32, precision=lax.Precision.HIGHEST)


def _mm_kernel(*refs, n_x, w_x, n_rows, n_tiles, n_out, epilogue, tm, sub_m):
    x_refs = refs[:n_x]
    w_refs = refs[n_x:n_x + len(w_x)]
    pos = n_x + len(w_x)
    row_refs = refs[pos:pos + n_rows]
    pos += n_rows
    tile_refs = refs[pos:pos + n_tiles]
    pos += n_tiles
    out_refs = refs[pos:pos + n_out]

    def sub(r, carry):
        rs = pl.ds(pl.multiple_of(r * sub_m, sub_m), sub_m)
        xs = [x[rs, :].astype(BF16) for x in x_refs]
        accs = [_dot(xs[xi], w[...]) for w, xi in zip(w_refs, w_x)]
        outs = epilogue(accs, [rr[...] for rr in row_refs], [t[rs, :] for t in tile_refs])
        for o_ref, o in zip(out_refs, outs):
            o_ref[rs, :] = o.astype(o_ref.dtype)
        return carry

    lax.fori_loop(0, tm // sub_m, sub, 0)


def fused_matmul(xs, ws, epilogue, out_dtypes, n_cols, *, rows=(), tiles=(), tm=1024, tn=512,
                 sub_m=256, name="mm"):
    m = xs[0].shape[0]
    tm = min(tm, m)
    sub_m = min(sub_m, tm)
    grid = (m // tm, n_cols // tn)
    in_specs = [pl.BlockSpec((tm, x.shape[1]), lambda i, j: (i, 0)) for x in xs]
    for w, _, off in ws:
        in_specs.append(pl.BlockSpec((w.shape[0], tn), functools.partial(lambda i, j, o: (0, j + o), o=off)))
    in_specs += [pl.BlockSpec((1, tn), lambda i, j: (0, j)) for _ in rows]
    in_specs += [pl.BlockSpec((tm, tn), lambda i, j: (i, j)) for _ in tiles]
    out_specs = [pl.BlockSpec((tm, tn), lambda i, j: (i, j)) for _ in out_dtypes]
    out_shape = [jax.ShapeDtypeStruct((m, n_cols), dt) for dt in out_dtypes]
    body = functools.partial(_mm_kernel, n_x=len(xs), w_x=tuple(xi for _, xi, _ in ws), n_rows=len(rows),
                             n_tiles=len(tiles), n_out=len(out_dtypes), epilogue=epilogue, tm=tm, sub_m=sub_m)
    return pl.pallas_call(
        body, grid=grid, in_specs=in_specs, out_specs=out_specs, out_shape=out_shape,
        compiler_params=_cparams("parallel", "arbitrary"), name=name,
    )(*xs, *[w for w, _, _ in ws], *rows, *tiles)


def _conv_mm_kernel(x_ref, w_ref, cw_ref, cb_ref, o_ref, ext_ref, carry_ref, *, tm, tn):
    i = pl.program_id(0)
    j = pl.program_id(1)
    cols = pl.ds(pl.multiple_of(j * tn, tn), tn)

    @pl.when(i == 0)
    def _():
        carry_ref[:, cols] = jnp.zeros((SUBLANES, tn), F32)

    ext_ref[0:SUBLANES, :] = carry_ref[:, cols]
    ext_ref[SUBLANES:SUBLANES + tm, :] = _dot(x_ref[...], w_ref[...])
    carry_ref[:, cols] = ext_ref[tm:tm + SUBLANES, :]
    acc = jnp.zeros((tm, tn), F32) + cb_ref[...]
    for k in range(SSM_CONV):
        start = SUBLANES - (SSM_CONV - 1) + k
        acc = acc + ext_ref[start:start + tm, :] * cw_ref[k:k + 1, :]
    o_ref[...] = _silu(acc)


def conv_matmul(x, w, conv_w, conv_b, *, tm=512, tn=512):
    m, k = x.shape
    n = w.shape[1]
    tm = min(tm, m)
    body = functools.partial(_conv_mm_kernel, tm=tm, tn=tn)
    return pl.pallas_call(
        body, grid=(m // tm, n // tn),
        in_specs=[pl.BlockSpec((tm, k), lambda i, j: (i, 0)),
                  pl.BlockSpec((k, tn), lambda i, j: (0, j)),
                  pl.BlockSpec((SSM_CONV, tn), lambda i, j: (0, j)),
                  pl.BlockSpec((1, tn), lambda i, j: (0, j))],
        out_specs=pl.BlockSpec((tm, tn), lambda i, j: (i, j)),
        out_shape=jax.ShapeDtypeStruct((m, n), F32),
        scratch_shapes=[pltpu.VMEM((tm + SUBLANES, tn), F32), pltpu.VMEM((SUBLANES, n), F32)],
        compiler_params=_cparams("arbitrary", "arbitrary"), name="conv_mm",
    )(x, w, conv_w, conv_b)


def _ln_mm_kernel(x_ref, w_ref, res_ref, g_ref, b_ref, o_ref, obf_ref, *, scale, nk):
    k = pl.program_id(1)
    part = _dot(x_ref[...], w_ref[...])

    @pl.when(k == 0)
    def _():
        o_ref[...] = part

    @pl.when(k > 0)
    def _():
        o_ref[...] += part

    @pl.when(k == nk - 1)
    def _():
        z = ALPHA * res_ref[...] + scale * o_ref[...]
        mu = jnp.mean(z, axis=-1, keepdims=True)
        zc = z - mu
        var = jnp.mean(zc * zc, axis=-1, keepdims=True)
        y = zc * lax.rsqrt(var + LN_EPS) * g_ref[...] + b_ref[...]
        o_ref[...] = y
        obf_ref[...] = y.astype(BF16)


def ln_matmul(x, w, resid, gamma, beta, scale, *, tm=256, tk=512):
    m, kdim = x.shape
    n = w.shape[1]
    tm = min(tm, m)
    nk = kdim // tk
    body = functools.partial(_ln_mm_kernel, scale=scale, nk=nk)
    return pl.pallas_call(
        body, grid=(m // tm, nk),
        in_specs=[pl.BlockSpec((tm, tk), lambda i, k: (i, k)),
                  pl.BlockSpec((tk, n), lambda i, k: (k, 0)),
                  pl.BlockSpec((tm, n), lambda i, k: (i, 0)),
                  pl.BlockSpec((1, n), lambda i, k: (0, 0)),
                  pl.BlockSpec((1, n), lambda i, k: (0, 0))],
        out_specs=[pl.BlockSpec((tm, n), lambda i, k: (i, 0)),
                   pl.BlockSpec((tm, n), lambda i, k: (i, 0))],
        out_shape=[jax.ShapeDtypeStruct((m, n), F32), jax.ShapeDtypeStruct((m, n), BF16)],
        compiler_params=_cparams("parallel", "arbitrary"), name="ln_mm",
    )(x, w, resid, gamma, beta)


def _lb_kernel(x_ref, lb_ref):
    x = x_ref[...]
    mx = jnp.max(x, axis=0, keepdims=True)
    e = jnp.exp(x - mx)
    p = e / jnp.sum(e, axis=0, keepdims=True)
    c = p[0:1, :]
    first = c
    lb_ref[0:1, :] = c - first
    for i in range(1, DEPTH):
        c = c + p[i:i + 1, :]
        lb_ref[i:i + 1, :] = c - first


def lower_bounds(logits):
    return pl.pallas_call(_lb_kernel, out_shape=jax.ShapeDtypeStruct(logits.shape, F32), name="hgrn_lb")(logits)


def _gla_kernel(q_ref, k_ref, v_ref, g_ref, gate_ref, gamma_ref, o_ref, st_ref, *, chunk, n_chunks, gate_act):
    dk = q_ref.shape[1]
    c = chunk
    nblk = c // SUBLANES

    @pl.when(pl.program_id(1) == 0)
    def _():
        st_ref[...] = jnp.zeros(st_ref.shape, F32)

    r2 = lax.broadcasted_iota(jnp.int32, (c, c), 0)
    c2 = lax.broadcasted_iota(jnp.int32, (c, c), 1)
    tril = (r2 >= c2).astype(F32)
    row = lax.broadcasted_iota(jnp.int32, (c, 1), 0)
    lane3 = lax.broadcasted_iota(jnp.int32, (nblk, SUBLANES, LANES), 2)
    blk3 = lax.broadcasted_iota(jnp.int32, (nblk, SUBLANES, LANES), 0)
    sub3 = lax.broadcasted_iota(jnp.int32, (nblk, SUBLANES, LANES), 1)
    gamma = gamma_ref[...]

    def step(ci, carry):
        rs = pl.ds(pl.multiple_of(ci * c, c), c)
        q = q_ref[rs, :]
        k = k_ref[rs, :]
        v = v_ref[rs, :]
        b = _dot_exact(tril, g_ref[rs, :])
        st = st_ref[...]
        vb = v.astype(BF16)

        o = _dot_nt((q * jnp.exp(b)).astype(BF16), st.astype(BF16))

        b3 = b.reshape(nblk, SUBLANES, dk)
        q3 = q.reshape(nblk, SUBLANES, dk)
        k3 = k.reshape(nblk, SUBLANES, dk)
        s3 = jnp.zeros((nblk, SUBLANES, LANES), F32)
        for j in range(SUBLANES):
            bj = b3[:, j:j + 1, :]
            kj = k3[:, j:j + 1, :]
            x = q3 * kj * jnp.exp(jnp.minimum(b3 - bj, 0.0))
            col = jnp.sum(x, axis=-1, keepdims=True)
            s3 = jnp.where((lane3 == blk3 * SUBLANES + j) & (sub3 >= j), col, s3)
        scores = s3.reshape(c, LANES)[:, :c]

        h = SUBLANES
        while h < c:
            ref = jnp.broadcast_to(b.reshape(c // (2 * h), 2 * h, dk)[:, h - 1:h, :],
                                   (c // (2 * h), 2 * h, dk)).reshape(c, dk)
            is_q = (row % (2 * h)) >= h
            e = jnp.exp(jnp.where(is_q, b - ref, ref - b))
            z = (jnp.where(is_q, q, k) * e).astype(BF16)
            p = _dot_nt(z, z)
            pair = ((r2 // (2 * h)) == (c2 // (2 * h))) & ((r2 % (2 * h)) >= h) & ((c2 % (2 * h)) < h)
            scores = scores + jnp.where(pair, p, 0.0)
            h *= 2

        o = o + _dot(scores.astype(BF16), vb)

        b_last = b[c - 1:c, :]
        kd = (k * jnp.exp(b_last - b)).astype(BF16)
        st_ref[...] = st * jnp.exp(b_last) + _dot_tn(vb, kd)

        ms = jnp.mean(o * o, axis=-1, keepdims=True)
        y = o * lax.rsqrt(ms + RMS_EPS) * gamma * gate_act(gate_ref[rs, :])
        o_ref[rs, :] = y.astype(o_ref.dtype)
        return carry

    lax.fori_loop(0, n_chunks, step, 0)


def gla_recurrence(q, k, v, g, gate, gamma, *, heads, dk, dv, gate_act, tb=512, chunk=GLA_CHUNK, name="gla"):
    t = q.shape[0]
    tb = min(tb, t)
    body = functools.partial(_gla_kernel, chunk=chunk, n_chunks=tb // chunk, gate_act=gate_act)
    kspec = pl.BlockSpec((tb, dk), lambda h, i: (i, h))
    vspec = pl.BlockSpec((tb, dv), lambda h, i: (i, h))
    return pl.pallas_call(
        body, grid=(heads, t // tb),
        in_specs=[kspec, kspec, vspec, kspec, vspec, pl.BlockSpec((1, dv), lambda h, i: (0, h))],
        out_specs=vspec,
        out_shape=jax.ShapeDtypeStruct((t, heads * dv), BF16),
        scratch_shapes=[pltpu.VMEM((dv, dk), F32)],
        compiler_params=_cparams("parallel", "arbitrary"), name=name,
    )(q, k, v, g, gate, gamma)


def _ssd_kernel(x_ref, b_ref, c_ref, z_ref, dt_ref, dtt_ref, bias_ref, biast_ref, a_ref, at_ref, d_ref,
                gamma_ref, o_ref, st_ref, *, n_chunks):
    c = SSD_CHUNK
    p = SSM_HEADDIM
    n = SSM_STATE
    hp = SSD_PAIR * SSM_HPG

    @pl.when(pl.program_id(1) == 0)
    def _():
        st_ref[...] = jnp.zeros(st_ref.shape, F32)

    r2 = lax.broadcasted_iota(jnp.int32, (c, c), 0)
    c2 = lax.broadcasted_iota(jnp.int32, (c, c), 1)
    causal = r2 >= c2
    tril = causal.astype(F32)
    triu = (r2 <= c2).astype(F32)
    bias = bias_ref[...]
    biast = biast_ref[...]
    a_row = a_ref[...]
    a_col = at_ref[...]
    d_row = d_ref[...]
    gamma = gamma_ref[...]

    def step(ci, carry):
        rs = pl.ds(pl.multiple_of(ci * c, c), c)
        dt = _softplus(dt_ref[rs, :] + bias)
        dtt = _softplus(dtt_ref[ci] + biast)
        cs = _dot_exact(tril, dt * a_row)
        cst = _dot_exact(dtt * a_col, triu)
        x = x_ref[rs, :]
        z = z_ref[rs, :]
        for gl in range(SSD_PAIR):
            bi = b_ref[rs, gl * n:(gl + 1) * n].astype(BF16)
            cmat = c_ref[rs, gl * n:(gl + 1) * n].astype(BF16)
            cb = _dot_nt(cmat, bi)
            ys = []
            ss = jnp.zeros((c, 1), F32)
            for hl in range(SSM_HPG):
                h = gl * SSM_HPG + hl
                cs_col = cs[:, h:h + 1]
                cs_row = cst[h:h + 1, :]
                lmat = jnp.where(causal, jnp.exp(jnp.minimum(cs_col - cs_row, 0.0)), 0.0)
                xh = x[:, h * p:(h + 1) * p]
                xdt = xh * dt[:, h:h + 1]
                st = st_ref[h]
                y = _dot((cb * lmat).astype(BF16), xdt.astype(BF16))
                y = y + _dot_nt(cmat, st.astype(BF16)) * jnp.exp(cs_col)
                last = cs[c - 1:c, h:h + 1]
                wdec = jnp.exp(last - cs_col)
                st_ref[h] = st * jnp.exp(last) + _dot_tn((xdt * wdec).astype(BF16), bi)
                y = (y + d_row[:, h:h + 1] * xh) * _silu(z[:, h * p:(h + 1) * p])
                ss = ss + jnp.sum(y * y, axis=-1, keepdims=True)
                ys.append(y)
            rinv = lax.rsqrt(ss * (1.0 / (SSM_HPG * p)) + RMS_EPS)
            for hl in range(SSM_HPG):
                h = gl * SSM_HPG + hl
                o_ref[rs, h * p:(h + 1) * p] = (ys[hl] * rinv * gamma[:, h * p:(h + 1) * p]).astype(o_ref.dtype)
        return carry

    lax.fori_loop(0, n_chunks, step, 0)


def ssd_recurrence(xbc, z, dt_raw, dt_bias, a_log, d_skip, gamma, *, tb=512):
    t = xbc.shape[0]
    tb = min(tb, t)
    c = SSD_CHUNK
    npair = SSM_GROUPS // SSD_PAIR
    hp = SSD_PAIR * SSM_HPG
    wx = hp * SSM_HEADDIM
    wn = SSD_PAIR * SSM_STATE
    dt_p = dt_raw.reshape(t, npair, hp).transpose(1, 0, 2)
    dtt_p = dt_raw.reshape(t // c, c, npair, hp).transpose(2, 0, 3, 1)
    a = -jnp.exp(a_log.astype(F32))
    rowp = lambda v_: v_.reshape(npair, 1, hp)
    colp = lambda v_: v_.reshape(npair, hp, 1)
    body = functools.partial(_ssd_kernel, n_chunks=tb // c)
    b_off = SSM_WIDTH // wn
    c_off = (SSM_WIDTH + SSM_GROUPS * SSM_STATE) // wn
    pspec_row = pl.BlockSpec((None, 1, hp), lambda g, i: (g, 0, 0))
    pspec_col = pl.BlockSpec((None, hp, 1), lambda g, i: (g, 0, 0))
    return pl.pallas_call(
        body, grid=(npair, t // tb),
        in_specs=[pl.BlockSpec((tb, wx), lambda g, i: (i, g)),
                  pl.BlockSpec((tb, wn), lambda g, i: (i, g + b_off)),
                  pl.BlockSpec((tb, wn), lambda g, i: (i, g + c_off)),
                  pl.BlockSpec((tb, wx), lambda g, i: (i, g)),
                  pl.BlockSpec((None, tb, hp), lambda g, i: (g, i, 0)),
                  pl.BlockSpec((None, tb // c, hp, c), lambda g, i: (g, i, 0, 0)),
                  pspec_row, pspec_col, pspec_row, pspec_col, pspec_row,
                  pl.BlockSpec((1, wx), lambda g, i: (0, g))],
        out_specs=pl.BlockSpec((tb, wx), lambda g, i: (i, g)),
        out_shape=jax.ShapeDtypeStruct((t, SSM_WIDTH), BF16),
        scratch_shapes=[pltpu.VMEM((hp, SSM_HEADDIM, SSM_STATE), F32)],
        compiler_params=_cparams("parallel", "arbitrary"), name="ssd",
    )(xbc, xbc, xbc, z, dt_p, dtt_p, rowp(dt_bias), colp(dt_bias), rowp(a), colp(a), rowp(d_skip),
      gamma.reshape(1, SSM_WIDTH))


def _ep_swiglu(accs, rows, tiles):
    return (_silu(accs[0]) * accs[1],)


def _ep_identity(accs, rows, tiles):
    return (accs[0],)


def _ep_scaled_silu(scale, accs, rows, tiles):
    return (_silu(accs[0]) * scale,)


def _ep_scale(scale, accs, rows, tiles):
    return (accs[0] * scale,)


def _ep_hgrn_forget(accs, rows, tiles):
    zf = accs[0]
    lb = rows[0]
    la = jnp.log(lb)
    lc = jnp.log1p(-lb) + _log_sigmoid(zf)
    log_f = jnp.maximum(la, lc) + jnp.log1p(jnp.exp(-jnp.abs(la - lc)))
    return (log_f, (1.0 - lb) * _sigmoid(-zf))


def _ep_log_alpha(accs, rows, tiles):
    return (_log_sigmoid(accs[0] + rows[0]) * (1.0 / GLA_TAU),)


def _ep_merge(accs, rows, tiles):
    return (_sigmoid(accs[0]) * accs[3] + _sigmoid(accs[1]) * accs[4] + _sigmoid(accs[2]) * accs[5],)


def _ep_ple(accs, rows, tiles):
    y = tiles[0] + _sigmoid(accs[0]) * accs[1]
    return (y, y)


def _ffn(h, hb, w_in, w_out, g, b):
    w_in = w_in.astype(BF16)
    mid = fused_matmul([hb], [(w_in, 0, 0), (w_in, 0, D_FF // 512)], _ep_swiglu, [BF16], D_FF,
                       name="ffn_in")[0]
    return ln_matmul(mid, w_out.astype(BF16), h, g[None, :], b[None, :], 0.5)


def _mixer(hb, lb, w_in, conv_w, conv_b, dt_bias, a_log, d_skip, gla_w_gate, gla_b_gate, mix_norm_g,
           w_branch):
    o = IN_OFFSETS
    seg = lambda s: w_in[:, o[s]:o[s + 1]].astype(BF16)
    proj = lambda s, ep, dt=F32, name="proj": fused_matmul(
        [hb], [(seg(s), 0, 0)], ep, [dt], IN_SPLITS[s], name=name)[0]

    hg_q = proj(0, functools.partial(_ep_scaled_silu, HG_DK ** -0.5))
    hg_logf, hg_k = fused_matmul([hb], [(seg(1), 0, 0)], _ep_hgrn_forget, [F32, F32], HG_WIDTH,
                                 rows=[lb[None, :]], name="proj_hgf")
    hg_i = proj(2, _ep_identity)
    hg_g = proj(3, _ep_identity)
    g_hg, g_ssm, g_gla = mix_norm_g[:HG_WIDTH], mix_norm_g[HG_WIDTH:HG_WIDTH + SSM_WIDTH], \
        mix_norm_g[HG_WIDTH + SSM_WIDTH:]
    y_hg = gla_recurrence(hg_q, hg_k, hg_i, hg_logf, hg_g, g_hg[None, :], heads=HG_HEADS, dk=HG_DK,
                          dv=HG_WIDTH // HG_HEADS, gate_act=_sigmoid, name="hgrn")

    ssm_z = proj(4, _ep_identity)
    xbc = conv_matmul(hb, seg(5), conv_w, conv_b[None, :])
    small_w = jnp.concatenate([w_in[:, o[6]:o[7]], w_in[:, o[11]:o[12]]], axis=1)
    small_w = jnp.pad(small_w, ((0, 0), (0, LANES - small_w.shape[1]))).astype(BF16)
    small = fused_matmul([hb], [(small_w, 0, 0)], _ep_identity, [F32], LANES, tn=LANES, name="proj_small")[0]
    y_ssm = ssd_recurrence(xbc, ssm_z, small[:, :SSM_HEADS], dt_bias, a_log, d_skip, g_ssm)

    gla_q = proj(7, functools.partial(_ep_scale, GLA_DK ** -0.5))
    gla_k = proj(8, _ep_identity)
    gla_v = proj(9, _ep_identity)
    gla_g = proj(10, _ep_identity)
    gate_w = jnp.zeros((LANES, GLA_HEADS * GLA_DK), F32).at[SSM_HEADS:SSM_HEADS + GLA_RANK].set(gla_w_gate)
    log_alpha = fused_matmul([small], [(gate_w.astype(BF16), 0, 0)], _ep_log_alpha, [F32], GLA_HEADS * GLA_DK,
                             rows=[gla_b_gate[None, :]], name="gla_gate")[0]
    y_gla = gla_recurrence(gla_q, gla_k, gla_v, log_alpha, gla_g, g_gla[None, :], heads=GLA_HEADS, dk=GLA_DK,
                           dv=GLA_DV, gate_act=_silu, name="gla")

    gates_w = seg(12)
    wb = w_branch.astype(BF16)
    nb = D_MODEL // 256
    merged = fused_matmul(
        [hb, y_hg, y_ssm, y_gla],
        [(gates_w, 0, 0), (gates_w, 0, nb), (gates_w, 0, 2 * nb),
         (wb[:HG_WIDTH], 1, 0), (wb[HG_WIDTH:HG_WIDTH + SSM_WIDTH], 2, 0), (wb[HG_WIDTH + SSM_WIDTH:], 3, 0)],
        _ep_merge, [BF16], D_MODEL, tm=512, tn=256, name="merge")[0]
    return merged


def kernel(x, p, ffn_w_in, ffn_w_out, ln_g, ln_b, w_in, hg_lb_logits, ssm_conv_w, ssm_conv_b, ssm_dt_bias,
           ssm_a_log, ssm_d, gla_w_gate, gla_b_gate, mix_norm_g, w_branch, w_out, ple_w_proj, ple_w_gate):
    bsz, t, d = x.shape
    lb_all = lower_bounds(hg_lb_logits.astype(F32))
    outs = []
    for bi in range(bsz):
        h = x[bi]
        hb = h.astype(BF16)
        for i in range(DEPTH):
            h, hb = _ffn(h, hb, ffn_w_in[i, 0], ffn_w_out[i, 0], ln_g[i, 0], ln_b[i, 0])
            merged = _mixer(hb, lb_all[i], w_in[i], ssm_conv_w[i], ssm_conv_b[i], ssm_dt_bias[i], ssm_a_log[i],
                            ssm_d[i], gla_w_gate[i], gla_b_gate[i], mix_norm_g[i], w_branch[i])
            h, hb = ln_matmul(merged, w_out[i].astype(BF16), h, ln_g[i, 1][None, :], ln_b[i, 1][None, :], 1.0)
            h, hb = _ffn(h, hb, ffn_w_in[i, 1], ffn_w_out[i, 1], ln_g[i, 2], ln_b[i, 2])
            h, hb = fused_matmul([hb, p[i, bi]], [(ple_w_gate[i].astype(BF16), 0, 0),
                                                  (ple_w_proj[i].astype(BF16), 1, 0)],
                                 _ep_ple, [F32, BF16], D_MODEL, tiles=[h], name="ple")
        outs.append(h)
    return jnp.stack(outs, axis=0)
```

```python
import collections
import functools

import jax
import jax.numpy as jnp
import numpy as np
from jax import lax
from jax.experimental import pallas as pl
from jax.experimental.pallas import tpu as pltpu

F32 = jnp.float32
BF16 = jnp.bfloat16

D_MODEL = 4096
DEPTH = 4
P_DIM = 256
D_FF = 5 * D_MODEL // 8
HG_WIDTH = 1536
HG_DK = 128
HG_HEADS = 12
SSM_WIDTH = 1536
SSM_HEADDIM = 64
SSM_HEADS = 24
SSM_GROUPS = 8
SSM_HPG = 3
SSM_STATE = 128
SSM_CONV = 4
SSM_XBC = SSM_WIDTH + 2 * SSM_GROUPS * SSM_STATE
GLA_WIDTH = 1024
GLA_HEADS = 4
GLA_DV = 256
GLA_DK = 128
GLA_RANK = 16
GLA_TAU = 16.0
IN_SPLITS = (HG_WIDTH, HG_WIDTH, HG_WIDTH, HG_WIDTH, SSM_WIDTH, SSM_XBC, SSM_HEADS,
             GLA_HEADS * GLA_DK, GLA_HEADS * GLA_DK, GLA_WIDTH, GLA_WIDTH, GLA_RANK, 3 * D_MODEL)
IN_OFFSETS = tuple(int(o) for o in np.cumsum((0,) + IN_SPLITS))
ALPHA = float((2 * DEPTH) ** 0.25)
LN_EPS = 1e-5
RMS_EPS = 1e-6
LOG2E = float(np.log2(np.e))

VMEM_LIMIT_BYTES = 56 * 1024 * 1024
LANES = 128
SUBLANES = 8

GLA_CHUNK = 64
SSD_CHUNK = 128
SSD_PAIR = 2

COL_HG_Q = 0
COL_HG_F = HG_WIDTH
COL_ID1 = 2 * HG_WIDTH
COL_XBC = COL_ID1 + 2 * HG_WIDTH + SSM_WIDTH
COL_ID2 = COL_XBC + SSM_XBC
COL_GATES = COL_ID2 + 2 * GLA_HEADS * GLA_DK + 2 * GLA_WIDTH
COL_SMALL = COL_GATES + 3 * D_MODEL
ID1_WIDTH = 2 * HG_WIDTH + SSM_WIDTH
ID2_WIDTH = 2 * GLA_HEADS * GLA_DK + 2 * GLA_WIDTH

W = collections.namedtuple("W", "w x col lead krows kblk", defaults=(0, (), None, 0))


def _cparams(*sem):
    return pltpu.CompilerParams(dimension_semantics=sem, vmem_limit_bytes=VMEM_LIMIT_BYTES)


def _sigmoid(x):
    return 1.0 / (1.0 + jnp.exp(-x))


def _silu(x):
    return x * _sigmoid(x)


def _log_sigmoid(x):
    return jnp.minimum(x, 0.0) - jnp.log1p(jnp.exp(-jnp.abs(x)))


def _softplus(x):
    return jnp.maximum(x, 0.0) + jnp.log1p(jnp.exp(-jnp.abs(x)))


def _dot(a, b):
    return jnp.dot(a, b, preferred_element_type=F32)


def _dot_nt(a, b):
    return lax.dot_general(a, b, (((1,), (1,)), ((), ())), preferred_element_type=F32)


def _dot_tn(a, b):
    return lax.dot_general(a, b, (((0,), (0,)), ((), ())), preferred_element_type=F32)


def _split3(x):
    x1 = x.astype(BF16)
    r1 = x - x1.astype(F32)
    x2 = r1.astype(BF16)
    x3 = (r1 - x2.astype(F32)).astype(BF16)
    return x1, x2, x3


def _prefix_rows(tril_bf16, x):
    x1, x2, x3 = _split3(x)
    return (_dot(tril_bf16, x1) + _dot(tril_bf16, x2)) + _dot(tril_bf16, x3)


def _prefix_lanes(x, triu_bf16):
    x1, x2, x3 = _split3(x)
    return (_dot(x1, triu_bf16) + _dot(x2, triu_bf16)) + _dot(x3, triu_bf16)


def _wspec(w, tn, nargs=2):
    lead = tuple(w.lead)
    krows = w.w.shape[-2] if w.krows is None else w.krows
    return pl.BlockSpec((None,) * len(lead) + (krows, tn),
                        lambda i, j: lead + (w.kblk, j + w.col))


def _mm_kernel(*refs, n_x, w_x, n_rows, n_tiles, n_out, epilogue, tm, sub_m):
    x_refs = refs[:n_x]
    w_refs = refs[n_x:n_x + len(w_x)]
    pos = n_x + len(w_x)
    row_refs = refs[pos:pos + n_rows]
    pos += n_rows
    tile_refs = refs[pos:pos + n_tiles]
    pos += n_tiles
    out_refs = refs[pos:pos + n_out]

    def sub(r, carry):
        rs = pl.ds(pl.multiple_of(r * sub_m, sub_m), sub_m)
        xs = [x[rs, :].astype(BF16) for x in x_refs]
        accs = [_dot(xs[xi], w[...]) for w, xi in zip(w_refs, w_x)]
        outs = epilogue(accs, [rr[...] for rr in row_refs], [t[rs, :] for t in tile_refs])
        for o_ref, o in zip(out_refs, outs):
            o_ref[rs, :] = o.astype(o_ref.dtype)
        return carry

    lax.fori_loop(0, tm // sub_m, sub, 0)


def fused_matmul(xs, ws, epilogue, out_dtypes, n_cols, *, rows=(), tiles=(), tm=1024, tn=512,
                 sub_m=256, name="mm"):
    m = xs[0].shape[0]
    tm = min(tm, m)
    sub_m = min(sub_m, tm)
    grid = (m // tm, n_cols // tn)
    in_specs = [pl.BlockSpec((tm, x.shape[1]), lambda i, j: (i, 0)) for x in xs]
    in_specs += [_wspec(w, tn) for w in ws]
    in_specs += [pl.BlockSpec((1, tn), lambda i, j: (0, j)) for _ in rows]
    in_specs += [pl.BlockSpec((tm, tn), lambda i, j: (i, j)) for _ in tiles]
    out_specs = [pl.BlockSpec((tm, tn), lambda i, j: (i, j)) for _ in out_dtypes]
    out_shape = [jax.ShapeDtypeStruct((m, n_cols), dt) for dt in out_dtypes]
    body = functools.partial(_mm_kernel, n_x=len(xs), w_x=tuple(w.x for w in ws), n_rows=len(rows),
                             n_tiles=len(tiles), n_out=len(out_dtypes), epilogue=epilogue, tm=tm, sub_m=sub_m)
    return pl.pallas_call(
        body, grid=grid, in_specs=in_specs, out_specs=out_specs, out_shape=out_shape,
        compiler_params=_cparams("parallel", "arbitrary"), name=name,
    )(*xs, *[w.w for w in ws], *rows, *tiles)


def _conv_mm_kernel(x_ref, w_ref, cw_ref, cb_ref, o_ref, ext_ref, carry_ref, *, tm, tn):
    i = pl.program_id(0)
    j = pl.program_id(1)
    cols = pl.ds(pl.multiple_of(j * tn, tn), tn)

    @pl.when(i == 0)
    def _():
        carry_ref[:, cols] = jnp.zeros((SUBLANES, tn), F32)

    ext_ref[0:SUBLANES, :] = carry_ref[:, cols]
    ext_ref[SUBLANES:SUBLANES + tm, :] = _dot(x_ref[...], w_ref[...])
    carry_ref[:, cols] = ext_ref[tm:tm + SUBLANES, :]
    acc = jnp.zeros((tm, tn), F32) + cb_ref[...]
    for k in range(SSM_CONV):
        start = SUBLANES - (SSM_CONV - 1) + k
        acc = acc + ext_ref[start:start + tm, :] * cw_ref[k:k + 1, :]
    o_ref[...] = _silu(acc)


def conv_matmul(x, w, conv_w, conv_b, n, *, tm=512, tn=512):
    m, k = x.shape
    tm = min(tm, m)
    body = functools.partial(_conv_mm_kernel, tm=tm, tn=tn)
    return pl.pallas_call(
        body, grid=(m // tm, n // tn),
        in_specs=[pl.BlockSpec((tm, k), lambda i, j: (i, 0)),
                  _wspec(w, tn),
                  pl.BlockSpec((SSM_CONV, tn), lambda i, j: (0, j)),
                  pl.BlockSpec((1, tn), lambda i, j: (0, j))],
        out_specs=pl.BlockSpec((tm, tn), lambda i, j: (i, j)),
        out_shape=jax.ShapeDtypeStruct((m, n), F32),
        scratch_shapes=[pltpu.VMEM((tm + SUBLANES, tn), F32), pltpu.VMEM((SUBLANES, n), F32)],
        compiler_params=_cparams("arbitrary", "arbitrary"), name="conv_mm",
    )(x, w.w, conv_w, conv_b)


def _ln_mm_kernel(x_ref, w_ref, res_ref, g_ref, b_ref, o_ref, obf_ref, *, scale, nj, tm, tn, sub_m, ln_m):
    j = pl.program_id(1)
    cols = pl.ds(pl.multiple_of(j * tn, tn), tn)

    def sub(r, carry):
        rs = pl.ds(pl.multiple_of(r * sub_m, sub_m), sub_m)
        o_ref[rs, cols] = ALPHA * res_ref[rs, :] + scale * _dot(x_ref[rs, :], w_ref[...])
        return carry

    lax.fori_loop(0, tm // sub_m, sub, 0)

    @pl.when(j == nj - 1)
    def _():
        g = g_ref[...]
        b = b_ref[...]

        def norm(r, carry):
            rs = pl.ds(pl.multiple_of(r * ln_m, ln_m), ln_m)
            z = o_ref[rs, :]
            mu = jnp.mean(z, axis=-1, keepdims=True)
            zc = z - mu
            var = jnp.mean(zc * zc, axis=-1, keepdims=True)
            y = zc * lax.rsqrt(var + LN_EPS) * g + b
            o_ref[rs, :] = y
            obf_ref[rs, :] = y.astype(BF16)
            return carry

        lax.fori_loop(0, tm // ln_m, norm, 0)


def ln_matmul(x, w, resid, gamma, beta, scale, *, tm=512, tn=512, sub_m=256, ln_m=32):
    m, kdim = x.shape
    n = resid.shape[1]
    tm = min(tm, m)
    nj = n // tn
    body = functools.partial(_ln_mm_kernel, scale=scale, nj=nj, tm=tm, tn=tn, sub_m=min(sub_m, tm), ln_m=ln_m)
    return pl.pallas_call(
        body, grid=(m // tm, nj),
        in_specs=[pl.BlockSpec((tm, kdim), lambda i, j: (i, 0)),
                  _wspec(w, tn),
                  pl.BlockSpec((tm, tn), lambda i, j: (i, j)),
                  pl.BlockSpec((1, n), lambda i, j: (0, 0)),
                  pl.BlockSpec((1, n), lambda i, j: (0, 0))],
        out_specs=[pl.BlockSpec((tm, n), lambda i, j: (i, 0)),
                   pl.BlockSpec((tm, n), lambda i, j: (i, 0))],
        out_shape=[jax.ShapeDtypeStruct((m, n), F32), jax.ShapeDtypeStruct((m, n), BF16)],
        compiler_params=_cparams("parallel", "arbitrary"), name="ln_mm",
    )(x, w.w, resid, gamma, beta)


def _lb_kernel(x_ref, lb_ref):
    x = x_ref[...]
    mx = jnp.max(x, axis=0, keepdims=True)
    e = jnp.exp(x - mx)
    p = e / jnp.sum(e, axis=0, keepdims=True)
    c = p[0:1, :]
    first = c
    lb_ref[0:1, :] = c - first
    for i in range(1, DEPTH):
        c = c + p[i:i + 1, :]
        lb_ref[i:i + 1, :] = c - first


def lower_bounds(logits):
    return pl.pallas_call(_lb_kernel, out_shape=jax.ShapeDtypeStruct(logits.shape, F32), name="hgrn_lb")(logits)


def _gla_levels(chunk):
    hs = []
    h = SUBLANES
    while h < chunk:
        hs.append(h)
        h *= 2
    return hs


def _gla_masks(chunk, dk):
    c = chunk
    t = np.arange(c)
    tril = (t[:, None] >= t[None, :]).astype(np.float32)
    pair, sgn = [], []
    for h in _gla_levels(c):
        is_q = (t % (2 * h)) >= h
        same = (t[:, None] // (2 * h)) == (t[None, :] // (2 * h))
        pair.append((same & is_q[:, None] & ~is_q[None, :]).astype(np.float32))
        sgn.append(np.broadcast_to(np.where(is_q, 1.0, -1.0)[:, None], (c, dk)).astype(np.float32))
    lane = np.arange(LANES)
    diag = np.stack([((lane[None, :] == (t[:, None] // SUBLANES) * SUBLANES + j)
                      & ((t[:, None] % SUBLANES) >= j)).astype(np.float32) for j in range(SUBLANES)])
    return (jnp.asarray(tril, BF16), jnp.asarray(np.stack(pair)), jnp.asarray(np.stack(sgn)),
            jnp.asarray(diag.reshape(SUBLANES, c // SUBLANES, SUBLANES, LANES)))


def _gla_kernel(q_ref, k_ref, v_ref, g_ref, gate_ref, gamma_ref, tril_ref, pair_ref, sgn_ref, diag_ref,
                o_ref, st_ref, *, chunk, n_chunks, hb, dk, dv, q_scale, gate_act):
    c = chunk
    nblk = c // SUBLANES
    levels = _gla_levels(c)

    @pl.when(pl.program_id(1) == 0)
    def _():
        st_ref[...] = jnp.zeros(st_ref.shape, F32)

    def step(ci, carry):
        rs = pl.ds(pl.multiple_of(ci * c, c), c)
        tril = tril_ref[...]
        for hh in range(hb):
            kc = slice(hh * dk, (hh + 1) * dk)
            vc = slice(hh * dv, (hh + 1) * dv)
            q = q_ref[rs, kc]
            if q_scale != 1.0:
                q = q * q_scale
            k = k_ref[rs, kc]
            vb = v_ref[rs, vc].astype(BF16)
            b2 = _prefix_rows(tril, g_ref[rs, kc]) * LOG2E
            st = st_ref[hh]

            o = _dot_nt((q * jnp.exp2(b2)).astype(BF16), st.astype(BF16))

            b3 = b2.reshape(nblk, SUBLANES, dk)
            q3 = q.reshape(nblk, SUBLANES, dk)
            k3 = k.reshape(nblk, SUBLANES, dk)
            s3 = jnp.zeros((nblk, SUBLANES, LANES), F32)
            for j in range(SUBLANES):
                x = q3 * k3[:, j:j + 1, :] * jnp.exp2(jnp.minimum(b3 - b3[:, j:j + 1, :], 0.0))
                col = jnp.sum(x, axis=-1, keepdims=True)
                s3 = jnp.where(diag_ref[j] > 0.0, col, s3)
            scores = s3.reshape(c, LANES)[:, :c]

            for li, h in enumerate(levels):
                ref = jnp.broadcast_to(b2.reshape(c // (2 * h), 2 * h, dk)[:, h - 1:h, :],
                                       (c // (2 * h), 2 * h, dk)).reshape(c, dk)
                sg = sgn_ref[li]
                z = (jnp.where(sg > 0.0, q, k) * jnp.exp2(sg * (b2 - ref))).astype(BF16)
                scores = scores + _dot_nt(z, z) * pair_ref[li]

            o = o + _dot(scores.astype(BF16), vb)

            b_last = b2[c - 1:c, :]
            kd = (k * jnp.exp2(b_last - b2)).astype(BF16)
            st_ref[hh] = st * jnp.exp2(b_last) + _dot_tn(vb, kd)

            ms = jnp.mean(o * o, axis=-1, keepdims=True)
            y = o * lax.rsqrt(ms + RMS_EPS) * gamma_ref[:, vc] * gate_act(gate_ref[rs, vc])
            o_ref[rs, vc] = y.astype(o_ref.dtype)
        return carry

    lax.fori_loop(0, n_chunks, step, 0)


def gla_recurrence(q, k, v, g, gate, gamma, *, heads, hb, dk, dv, gate_act, q_scale=1.0, offs=(0, 0, 0, 0, 0),
                   tb=512, chunk=GLA_CHUNK, name="gla"):
    t = q.shape[0]
    tb = min(tb, t)
    masks = _gla_masks(chunk, dk)
    body = functools.partial(_gla_kernel, chunk=chunk, n_chunks=tb // chunk, hb=hb, dk=dk, dv=dv,
                             q_scale=q_scale, gate_act=gate_act)
    kspec = lambda off: pl.BlockSpec((tb, hb * dk), lambda h, i: (i, h + off))
    vspec = lambda off: pl.BlockSpec((tb, hb * dv), lambda h, i: (i, h + off))
    full = lambda a: pl.BlockSpec(a.shape, lambda h, i: (0,) * a.ndim)
    return pl.pallas_call(
        body, grid=(heads // hb, t // tb),
        in_specs=[kspec(offs[0]), kspec(offs[1]), vspec(offs[2]), kspec(offs[3]), vspec(offs[4]),
                  pl.BlockSpec((1, hb * dv), lambda h, i: (0, h))] + [full(a) for a in masks],
        out_specs=vspec(0),
        out_shape=jax.ShapeDtypeStruct((t, heads * dv), BF16),
        scratch_shapes=[pltpu.VMEM((hb, dv, dk), F32)],
        compiler_params=_cparams("parallel", "arbitrary"), name=name,
    )(q, k, v, g, gate, gamma, *masks)


def _ssd_kernel(x_ref, b_ref, c_ref, z_ref, dt_ref, dtt_ref, bias_ref, biast_ref, a_ref, at_ref, d_ref,
                gamma_ref, o_ref, st_ref, *, n_chunks):
    c = SSD_CHUNK
    p = SSM_HEADDIM
    n = SSM_STATE
    hp = SSD_PAIR * SSM_HPG

    @pl.when(pl.program_id(1) == 0)
    def _():
        st_ref[...] = jnp.zeros(st_ref.shape, F32)

    r2 = lax.broadcasted_iota(jnp.int32, (c, c), 0)
    c2 = lax.broadcasted_iota(jnp.int32, (c, c), 1)
    causal = r2 >= c2
    tril = causal.astype(BF16)
    triu = (r2 <= c2).astype(BF16)
    bias = bias_ref[...]
    biast = biast_ref[...]
    a_row = a_ref[...] * LOG2E
    a_col = at_ref[...] * LOG2E
    d_row = d_ref[...]

    def step(ci, carry):
        rs = pl.ds(pl.multiple_of(ci * c, c), c)
        dt = _softplus(dt_ref[rs, :] + bias)
        dtt = _softplus(dtt_ref[ci] + biast)
        cs = _prefix_rows(tril, dt * a_row)
        cst = _prefix_lanes(dtt * a_col, triu)
        for gl in range(SSD_PAIR):
            bi = b_ref[rs, gl * n:(gl + 1) * n].astype(BF16)
            cmat = c_ref[rs, gl * n:(gl + 1) * n].astype(BF16)
            cb = _dot_nt(cmat, bi)
            ys = []
            ss = jnp.zeros((c, 1), F32)
            for hl in range(SSM_HPG):
                h = gl * SSM_HPG + hl
                hc = slice(h * p, (h + 1) * p)
                cs_b = jnp.broadcast_to(cs[:, h:h + 1], (c, c))
                dt_b = jnp.broadcast_to(dt[:, h:h + 1], (c, c))
                lmat = jnp.where(causal, jnp.exp2(jnp.minimum(cs_b - cst[h:h + 1, :], 0.0)), 0.0)
                xh = x_ref[rs, hc]
                st = st_ref[h]
                y = _dot((cb * lmat * dtt[h:h + 1, :]).astype(BF16), xh.astype(BF16))
                y = y + _dot_nt(cmat, st.astype(BF16)) * jnp.exp2(cs_b)[:, :p]
                last = cs[c - 1:c, h:h + 1]
                u = dt_b * jnp.exp2(last - cs_b)
                st_ref[h] = st * jnp.exp2(last) + _dot_tn((xh * u[:, :p]).astype(BF16), bi)
                y = (y + d_row[:, h:h + 1] * xh) * _silu(z_ref[rs, hc])
                ss = ss + jnp.sum(y * y, axis=-1, keepdims=True)
                ys.append(y)
            rinv = lax.rsqrt(ss * (1.0 / (SSM_HPG * p)) + RMS_EPS)
            for hl in range(SSM_HPG):
                hc = slice((gl * SSM_HPG + hl) * p, (gl * SSM_HPG + hl + 1) * p)
                o_ref[rs, hc] = (ys[hl] * rinv * gamma_ref[:, hc]).astype(o_ref.dtype)
        return carry

    lax.fori_loop(0, n_chunks, step, 0)


def ssd_recurrence(xbc, z, z_off, dt_raw, dt_bias, a_log, d_skip, gamma, *, tb=512):
    t = xbc.shape[0]
    tb = min(tb, t)
    c = SSD_CHUNK
    npair = SSM_GROUPS // SSD_PAIR
    hp = SSD_PAIR * SSM_HPG
    wx = hp * SSM_HEADDIM
    wn = SSD_PAIR * SSM_STATE
    dt_p = dt_raw.reshape(t, npair, hp).transpose(1, 0, 2)
    dtt_p = dt_raw.reshape(t // c, c, npair, hp).transpose(2, 0, 3, 1)
    a = -jnp.exp(a_log.astype(F32))
    rowp = lambda v_: v_.reshape(npair, 1, hp)
    colp = lambda v_: v_.reshape(npair, hp, 1)
    body = functools.partial(_ssd_kernel, n_chunks=tb // c)
    b_off = SSM_WIDTH // wn
    c_off = (SSM_WIDTH + SSM_GROUPS * SSM_STATE) // wn
    pspec_row = pl.BlockSpec((None, 1, hp), lambda g, i: (g, 0, 0))
    pspec_col = pl.BlockSpec((None, hp, 1), lambda g, i: (g, 0, 0))
    return pl.pallas_call(
        body, grid=(npair, t // tb),
        in_specs=[pl.BlockSpec((tb, wx), lambda g, i: (i, g)),
                  pl.BlockSpec((tb, wn), lambda g, i: (i, g + b_off)),
                  pl.BlockSpec((tb, wn), lambda g, i: (i, g + c_off)),
                  pl.BlockSpec((tb, wx), lambda g, i: (i, g + z_off)),
                  pl.BlockSpec((None, tb, hp), lambda g, i: (g, i, 0)),
                  pl.BlockSpec((None, tb // c, hp, c), lambda g, i: (g, i, 0, 0)),
                  pspec_row, pspec_col, pspec_row, pspec_col, pspec_row,
                  pl.BlockSpec((1, wx), lambda g, i: (0, g))],
        out_specs=pl.BlockSpec((tb, wx), lambda g, i: (i, g)),
        out_shape=jax.ShapeDtypeStruct((t, SSM_WIDTH), BF16),
        scratch_shapes=[pltpu.VMEM((hp, SSM_HEADDIM, SSM_STATE), F32)],
        compiler_params=_cparams("parallel", "arbitrary"), name="ssd",
    )(xbc, xbc, xbc, z, dt_p, dtt_p, rowp(dt_bias), colp(dt_bias), rowp(a), colp(a), rowp(d_skip),
      gamma.reshape(1, SSM_WIDTH))


def _ep_swiglu(accs, rows, tiles):
    return (_silu(accs[0]) * accs[1],)


def _ep_identity(accs, rows, tiles):
    return (accs[0],)


def _ep_scaled_silu(scale, accs, rows, tiles):
    return (_silu(accs[0]) * scale,)


def _ep_hgrn_forget(accs, rows, tiles):
    zf = accs[0]
    lb = rows[0]
    la = jnp.log(lb)
    lc = jnp.log1p(-lb) + _log_sigmoid(zf)
    log_f = jnp.maximum(la, lc) + jnp.log1p(jnp.exp(-jnp.abs(la - lc)))
    return (log_f, (1.0 - lb) * _sigmoid(-zf))


def _ep_log_alpha(accs, rows, tiles):
    return (_log_sigmoid(accs[0] + rows[0]) * (1.0 / GLA_TAU),)


def _ep_merge(accs, rows, tiles):
    return (_sigmoid(accs[0]) * accs[3] + _sigmoid(accs[1]) * accs[4] + _sigmoid(accs[2]) * accs[5],)


def _ep_ple(accs, rows, tiles):
    y = tiles[0] + _sigmoid(accs[0]) * accs[1]
    return (y, y)


def _regroup_w_in(w_in):
    o = IN_OFFSETS
    small = jnp.concatenate([w_in[..., o[6]:o[7]], w_in[..., o[11]:o[12]]], axis=-1)
    small = jnp.pad(small, ((0, 0), (0, 0), (0, LANES - small.shape[-1])))
    return jnp.concatenate([w_in[..., :o[6]], w_in[..., o[7]:o[11]], w_in[..., o[12]:], small],
                           axis=-1).astype(BF16)


def _ffn(h, hb, w_in, w_out, lead, g, b):
    mid = fused_matmul([hb], [W(w_in, 0, 0, lead), W(w_in, 0, D_FF // 512, lead)], _ep_swiglu, [BF16], D_FF,
                       name="ffn_in")[0]
    return ln_matmul(mid, W(w_out, 0, 0, lead), h, g[None, :], b[None, :], 0.5)


def _mixer(hb, layer, lb, w_all, conv_w, conv_b, dt_bias, a_log, d_skip, gate_w, gla_b_gate, mix_norm_g, w_branch):
    lead = (layer,)
    tn = 512
    wcol = lambda col, t_=tn: W(w_all, 0, col // t_, lead)
    g_hg, g_ssm, g_gla = mix_norm_g[:HG_WIDTH], mix_norm_g[HG_WIDTH:HG_WIDTH + SSM_WIDTH], \
        mix_norm_g[HG_WIDTH + SSM_WIDTH:]

    hg_q = fused_matmul([hb], [wcol(COL_HG_Q)], functools.partial(_ep_scaled_silu, HG_DK ** -0.5), [F32],
                        HG_WIDTH, name="proj_hgq")[0]
    hg_logf, hg_k = fused_matmul([hb], [wcol(COL_HG_F)], _ep_hgrn_forget, [F32, F32], HG_WIDTH,
                                 rows=[lb[None, :]], name="proj_hgf")
    id1 = fused_matmul([hb], [wcol(COL_ID1)], _ep_identity, [F32], ID1_WIDTH, name="proj_id1")[0]
    xbc = conv_matmul(hb, wcol(COL_XBC), conv_w, conv_b[None, :], SSM_XBC)
    small = fused_matmul([hb], [wcol(COL_SMALL, LANES)], _ep_identity, [F32], LANES, tn=LANES,
                         name="proj_small")[0]
    id2 = fused_matmul([hb], [wcol(COL_ID2)], _ep_identity, [F32], ID2_WIDTH, name="proj_id2")[0]
    log_alpha = fused_matmul([small], [W(gate_w, 0, 0, lead)], _ep_log_alpha, [F32], GLA_HEADS * GLA_DK,
                             rows=[gla_b_gate[None, :]], name="gla_gate")[0]

    hb_hg = 6
    y_hg = gla_recurrence(hg_q, hg_k, id1, hg_logf, id1, g_hg[None, :], heads=HG_HEADS, hb=hb_hg, dk=HG_DK,
                          dv=HG_DK, gate_act=_sigmoid, offs=(0, 0, 0, 0, HG_WIDTH // (hb_hg * HG_DK)),
                          name="hgrn")
    y_ssm = ssd_recurrence(xbc, id1, 2 * HG_WIDTH // (SSD_PAIR * SSM_HPG * SSM_HEADDIM), small[:, :SSM_HEADS],
                           dt_bias, a_log, d_skip, g_ssm)
    hb_gla = 4
    wk, wv = hb_gla * GLA_DK, hb_gla * GLA_DV
    y_gla = gla_recurrence(id2, id2, id2, log_alpha, id2, g_gla[None, :], heads=GLA_HEADS, hb=hb_gla, dk=GLA_DK,
                           dv=GLA_DV, gate_act=_silu, q_scale=GLA_DK ** -0.5,
                           offs=(0, GLA_HEADS * GLA_DK // wk, 2 * GLA_HEADS * GLA_DK // wv, 0,
                                 (2 * GLA_HEADS * GLA_DK + GLA_WIDTH) // wv), name="gla")

    tn_m = 256
    nb = D_MODEL // tn_m
    gcol = COL_GATES // tn_m
    merged = fused_matmul(
        [hb, y_hg, y_ssm, y_gla],
        [W(w_all, 0, gcol, lead), W(w_all, 0, gcol + nb, lead), W(w_all, 0, gcol + 2 * nb, lead),
         W(w_branch, 1, 0, lead, HG_WIDTH, 0), W(w_branch, 2, 0, lead, SSM_WIDTH, 1),
         W(w_branch, 3, 0, lead, GLA_WIDTH, (HG_WIDTH + SSM_WIDTH) // GLA_WIDTH)],
        _ep_merge, [BF16], D_MODEL, tm=512, tn=tn_m, name="merge")[0]
    return merged


def kernel(x, p, ffn_w_in, ffn_w_out, ln_g, ln_b, w_in, hg_lb_logits, ssm_conv_w, ssm_conv_b, ssm_dt_bias,
           ssm_a_log, ssm_d, gla_w_gate, gla_b_gate, mix_norm_g, w_branch, w_out, ple_w_proj, ple_w_gate):
    bsz, t, d = x.shape
    lb_all = lower_bounds(hg_lb_logits.astype(F32))
    ffn_w_in_b = ffn_w_in.astype(BF16)
    ffn_w_out_b = ffn_w_out.astype(BF16)
    w_all = _regroup_w_in(w_in)
    w_branch_b = w_branch.astype(BF16)
    w_out_b = w_out.astype(BF16)
    ple_w_gate_b = ple_w_gate.astype(BF16)
    ple_w_proj_b = ple_w_proj.astype(BF16)
    gate_w = jnp.zeros((DEPTH, LANES, GLA_HEADS * GLA_DK), F32).at[:, SSM_HEADS:SSM_HEADS + GLA_RANK].set(
        gla_w_gate).astype(BF16)
    outs = []
    for bi in range(bsz):
        h = x[bi]
        hb = h.astype(BF16)
        for i in range(DEPTH):
            h, hb = _ffn(h, hb, ffn_w_in_b, ffn_w_out_b, (i, 0), ln_g[i, 0], ln_b[i, 0])
            merged = _mixer(hb, i, lb_all[i], w_all, ssm_conv_w[i], ssm_conv_b[i], ssm_dt_bias[i], ssm_a_log[i],
                            ssm_d[i], gate_w, gla_b_gate[i], mix_norm_g[i], w_branch_b)
            h, hb = ln_matmul(merged, W(w_out_b, 0, 0, (i,)), h, ln_g[i, 1][None, :], ln_b[i, 1][None, :], 1.0)
            h, hb = _ffn(h, hb, ffn_w_in_b, ffn_w_out_b, (i, 1), ln_g[i, 2], ln_b[i, 2])
            h, hb = fused_matmul([hb, p[i, bi]], [W(ple_w_gate_b, 0, 0, (i,)), W(ple_w_proj_b, 1, 0, (i,))],
                                 _ep_ple, [F32, BF16], D_MODEL, tiles=[h], name="ple")
        outs.append(h)
    return jnp.stack(outs, axis=0)
```

```python
import collections
import functools

import jax
import jax.numpy as jnp
import numpy as np
from jax import lax
from jax.experimental import pallas as pl
from jax.experimental.pallas import tpu as pltpu

F32 = jnp.float32
BF16 = jnp.bfloat16

D_MODEL = 4096
DEPTH = 4
P_DIM = 256
D_FF = 5 * D_MODEL // 8
HG_WIDTH = 1536
HG_DK = 128
HG_HEADS = 12
SSM_WIDTH = 1536
SSM_HEADDIM = 64
SSM_HEADS = 24
SSM_GROUPS = 8
SSM_HPG = 3
SSM_STATE = 128
SSM_CONV = 4
SSM_XBC = SSM_WIDTH + 2 * SSM_GROUPS * SSM_STATE
GLA_WIDTH = 1024
GLA_HEADS = 4
GLA_DV = 256
GLA_DK = 128
GLA_RANK = 16
GLA_TAU = 16.0
IN_SPLITS = (HG_WIDTH, HG_WIDTH, HG_WIDTH, HG_WIDTH, SSM_WIDTH, SSM_XBC, SSM_HEADS,
             GLA_HEADS * GLA_DK, GLA_HEADS * GLA_DK, GLA_WIDTH, GLA_WIDTH, GLA_RANK, 3 * D_MODEL)
IN_OFFSETS = tuple(int(o) for o in np.cumsum((0,) + IN_SPLITS))
ALPHA = float((2 * DEPTH) ** 0.25)
LN_EPS = 1e-5
RMS_EPS = 1e-6
LOG2E = float(np.log2(np.e))

VMEM_LIMIT_BYTES = 56 * 1024 * 1024
LANES = 128
SUBLANES = 8

GLA_CHUNK = 64
SSD_CHUNK = 128
SSD_PAIR = 2

COL_HG_Q = 0
COL_HG_F = HG_WIDTH
COL_ID1 = 2 * HG_WIDTH
COL_XBC = COL_ID1 + 2 * HG_WIDTH + SSM_WIDTH
ID1_WIDTH = 2 * HG_WIDTH + SSM_WIDTH
ID2_WIDTH = 2 * GLA_HEADS * GLA_DK + 2 * GLA_WIDTH

W = collections.namedtuple("W", "w x col lead krows kblk", defaults=(0, (), None, 0))


def _cparams(*sem):
    return pltpu.CompilerParams(dimension_semantics=sem, vmem_limit_bytes=VMEM_LIMIT_BYTES)


def _sigmoid(x):
    return 1.0 / (1.0 + jnp.exp(-x))


def _silu(x):
    return x * _sigmoid(x)


def _log_sigmoid(x):
    return jnp.minimum(x, 0.0) - jnp.log1p(jnp.exp(-jnp.abs(x)))


def _softplus(x):
    return jnp.maximum(x, 0.0) + jnp.log1p(jnp.exp(-jnp.abs(x)))


def _dot(a, b):
    return jnp.dot(a, b, preferred_element_type=F32)


def _dot_nt(a, b):
    return lax.dot_general(a, b, (((1,), (1,)), ((), ())), preferred_element_type=F32)


def _dot_tn(a, b):
    return lax.dot_general(a, b, (((0,), (0,)), ((), ())), preferred_element_type=F32)


def _split3(x):
    x1 = x.astype(BF16)
    r1 = x - x1.astype(F32)
    x2 = r1.astype(BF16)
    x3 = (r1 - x2.astype(F32)).astype(BF16)
    return x1, x2, x3


def _prefix_rows(tril_bf16, x):
    x1, x2, x3 = _split3(x)
    return (_dot(tril_bf16, x1) + _dot(tril_bf16, x2)) + _dot(tril_bf16, x3)


def _prefix_lanes(x, triu_bf16):
    x1, x2, x3 = _split3(x)
    return (_dot(x1, triu_bf16) + _dot(x2, triu_bf16)) + _dot(x3, triu_bf16)


def _wspec(w, tn, nargs=2):
    lead = tuple(w.lead)
    krows = w.w.shape[-2] if w.krows is None else w.krows
    return pl.BlockSpec((None,) * len(lead) + (krows, tn),
                        lambda i, j: lead + (w.kblk, j + w.col))


def _mm_kernel(*refs, n_x, w_x, n_rows, n_tiles, n_out, epilogue, tm, sub_m):
    x_refs = refs[:n_x]
    w_refs = refs[n_x:n_x + len(w_x)]
    pos = n_x + len(w_x)
    row_refs = refs[pos:pos + n_rows]
    pos += n_rows
    tile_refs = refs[pos:pos + n_tiles]
    pos += n_tiles
    out_refs = refs[pos:pos + n_out]

    def sub(r, carry):
        rs = pl.ds(pl.multiple_of(r * sub_m, sub_m), sub_m)
        xs = [x[rs, :].astype(BF16) for x in x_refs]
        accs = [_dot(xs[xi], w[...]) for w, xi in zip(w_refs, w_x)]
        outs = epilogue(accs, [rr[...] for rr in row_refs], [t[rs, :] for t in tile_refs])
        for o_ref, o in zip(out_refs, outs):
            o_ref[rs, :] = o.astype(o_ref.dtype)
        return carry

    lax.fori_loop(0, tm // sub_m, sub, 0, unroll=True)


def fused_matmul(xs, ws, epilogue, out_dtypes, n_cols, *, rows=(), tiles=(), tm=1024, tn=512,
                 sub_m=256, name="mm"):
    m = xs[0].shape[0]
    tm = min(tm, m)
    sub_m = min(sub_m, tm)
    grid = (m // tm, n_cols // tn)
    in_specs = [pl.BlockSpec((tm, x.shape[1]), lambda i, j: (i, 0)) for x in xs]
    in_specs += [_wspec(w, tn) for w in ws]
    in_specs += [pl.BlockSpec((1, tn), lambda i, j: (0, j)) for _ in rows]
    in_specs += [pl.BlockSpec((tm, tn), lambda i, j: (i, j)) for _ in tiles]
    out_specs = [pl.BlockSpec((tm, tn), lambda i, j: (i, j)) for _ in out_dtypes]
    out_shape = [jax.ShapeDtypeStruct((m, n_cols), dt) for dt in out_dtypes]
    body = functools.partial(_mm_kernel, n_x=len(xs), w_x=tuple(w.x for w in ws), n_rows=len(rows),
                             n_tiles=len(tiles), n_out=len(out_dtypes), epilogue=epilogue, tm=tm, sub_m=sub_m)
    return pl.pallas_call(
        body, grid=grid, in_specs=in_specs, out_specs=out_specs, out_shape=out_shape,
        compiler_params=_cparams("parallel", "arbitrary"), name=name,
    )(*xs, *[w.w for w in ws], *rows, *tiles)


def _conv_mm_kernel(x_ref, w_ref, cw_ref, cb_ref, o_ref, ext_ref, carry_ref, *, tm, tn):
    i = pl.program_id(0)
    j = pl.program_id(1)
    cols = pl.ds(pl.multiple_of(j * tn, tn), tn)

    @pl.when(i == 0)
    def _():
        carry_ref[:, cols] = jnp.zeros((SUBLANES, tn), F32)

    ext_ref[0:SUBLANES, :] = carry_ref[:, cols]
    ext_ref[SUBLANES:SUBLANES + tm, :] = _dot(x_ref[...], w_ref[...])
    carry_ref[:, cols] = ext_ref[tm:tm + SUBLANES, :]
    acc = jnp.zeros((tm, tn), F32) + cb_ref[...]
    for k in range(SSM_CONV):
        start = SUBLANES - (SSM_CONV - 1) + k
        acc = acc + ext_ref[start:start + tm, :] * cw_ref[k:k + 1, :]
    o_ref[...] = _silu(acc)


def conv_matmul(x, w, conv_w, conv_b, n, *, tm=512, tn=512):
    m, k = x.shape
    tm = min(tm, m)
    body = functools.partial(_conv_mm_kernel, tm=tm, tn=tn)
    return pl.pallas_call(
        body, grid=(m // tm, n // tn),
        in_specs=[pl.BlockSpec((tm, k), lambda i, j: (i, 0)),
                  _wspec(w, tn),
                  pl.BlockSpec((SSM_CONV, tn), lambda i, j: (0, j)),
                  pl.BlockSpec((1, tn), lambda i, j: (0, j))],
        out_specs=pl.BlockSpec((tm, tn), lambda i, j: (i, j)),
        out_shape=jax.ShapeDtypeStruct((m, n), F32),
        scratch_shapes=[pltpu.VMEM((tm + SUBLANES, tn), F32), pltpu.VMEM((SUBLANES, n), F32)],
        compiler_params=_cparams("arbitrary", "arbitrary"), name="conv_mm",
    )(x, w.w, conv_w, conv_b)


def _ln_kernel(z_ref, g_ref, b_ref, o_ref, obf_ref, *, tr, ln_m):
    g = g_ref[...]
    b = b_ref[...]

    def norm(r, carry):
        rs = pl.ds(pl.multiple_of(r * ln_m, ln_m), ln_m)
        z = z_ref[rs, :]
        mu = jnp.mean(z, axis=-1, keepdims=True)
        zc = z - mu
        var = jnp.mean(zc * zc, axis=-1, keepdims=True)
        y = zc * lax.rsqrt(var + LN_EPS) * g + b
        o_ref[rs, :] = y
        obf_ref[rs, :] = y.astype(BF16)
        return carry

    lax.fori_loop(0, tr // ln_m, norm, 0)


def layer_norm_rows(z, gamma, beta, *, tr=256, ln_m=16):
    m, n = z.shape
    tr = min(tr, m)
    row = pl.BlockSpec((tr, n), lambda i: (i, 0))
    vec = pl.BlockSpec((1, n), lambda i: (0, 0))
    return pl.pallas_call(
        functools.partial(_ln_kernel, tr=tr, ln_m=ln_m), grid=(m // tr,),
        in_specs=[row, vec, vec], out_specs=[row, row],
        out_shape=[jax.ShapeDtypeStruct((m, n), F32), jax.ShapeDtypeStruct((m, n), BF16)],
        compiler_params=_cparams("parallel"), name="ln_rows",
    )(z, gamma, beta)


def _ep_resid(scale, accs, rows, tiles):
    return (ALPHA * tiles[0] + scale * accs[0],)


def ln_matmul(x, w, resid, gamma, beta, scale):
    z = fused_matmul([x], [w], functools.partial(_ep_resid, scale), [F32], resid.shape[1], tiles=[resid],
                     name="resid_mm")[0]
    return layer_norm_rows(z, gamma, beta)


def _lb_kernel(x_ref, lb_ref):
    x = x_ref[...]
    mx = jnp.max(x, axis=0, keepdims=True)
    e = jnp.exp(x - mx)
    p = e / jnp.sum(e, axis=0, keepdims=True)
    c = p[0:1, :]
    first = c
    lb_ref[0:1, :] = c - first
    for i in range(1, DEPTH):
        c = c + p[i:i + 1, :]
        lb_ref[i:i + 1, :] = c - first


def lower_bounds(logits):
    return pl.pallas_call(_lb_kernel, out_shape=jax.ShapeDtypeStruct(logits.shape, F32), name="hgrn_lb")(logits)


def _gla_levels(chunk):
    hs = []
    h = SUBLANES
    while h < chunk:
        hs.append(h)
        h *= 2
    return hs


def _gla_masks(chunk, dk):
    c = chunk
    t = np.arange(c)
    tril = (t[:, None] >= t[None, :]).astype(np.float32)
    pair, sgn = [], []
    for h in _gla_levels(c):
        is_q = (t % (2 * h)) >= h
        same = (t[:, None] // (2 * h)) == (t[None, :] // (2 * h))
        pair.append((same & is_q[:, None] & ~is_q[None, :]).astype(np.float32))
        sgn.append(np.broadcast_to(np.where(is_q, 1.0, -1.0)[:, None], (c, dk)).astype(np.float32))
    lane = np.arange(LANES)
    esel = np.tile(((lane[None, :] % SUBLANES) == np.arange(SUBLANES)[:, None]).astype(np.float32)[:, None, :],
                   (1, dk, 1)).reshape(SUBLANES * dk, LANES)
    diag = ((lane[None, :] < c) & (lane[None, :] // SUBLANES == t[:, None] // SUBLANES)
            & (t[:, None] % SUBLANES >= lane[None, :] % SUBLANES)).astype(np.float32)
    return (jnp.asarray(tril, BF16), jnp.asarray(np.stack(pair)), jnp.asarray(np.stack(sgn)),
            jnp.asarray(esel, BF16), jnp.asarray(diag))


def _gla_kernel(q_ref, k_ref, v_ref, g_ref, gate_ref, gamma_ref, tril_ref, pair_ref, sgn_ref, esel_ref, diag_ref,
                o_ref, st_ref, bs_ref, *, chunk, n_chunks, hb, dk, dv, q_scale, gate_act):
    c = chunk
    nblk = c // SUBLANES
    levels = _gla_levels(c)

    @pl.when(pl.program_id(1) == 0)
    def _():
        st_ref[...] = jnp.zeros(st_ref.shape, F32)

    def step(ci, carry):
        base = pl.multiple_of(ci * c, c)
        rs = pl.ds(base, c)
        tril = tril_ref[...]
        heads = range(hb)
        kcs = [slice(hh * dk, (hh + 1) * dk) for hh in heads]
        vcs = [slice(hh * dv, (hh + 1) * dv) for hh in heads]

        def load_q(hh):
            q = q_ref[rs, kcs[hh]]
            return q * q_scale if q_scale != 1.0 else q

        for hh in heads:
            bs_ref[hh, 0] = _prefix_rows(tril, g_ref[rs, kcs[hh]]) * LOG2E
            bs_ref[hh, 1] = k_ref[rs, kcs[hh]]

        o_inter = []
        for hh in heads:
            b2 = bs_ref[hh, 0]
            o_inter.append(_dot_nt((load_q(hh) * jnp.exp2(b2)).astype(BF16), st_ref[hh].astype(BF16)))

        scores = []
        for hh in heads:
            q = load_q(hh)
            b2 = bs_ref[hh, 0]
            xs = []
            for j in range(SUBLANES):
                blocks = []
                for p in range(nblk):
                    r = p * SUBLANES + j
                    rows = slice(p * SUBLANES, (p + 1) * SUBLANES)
                    bj = bs_ref[hh, 0, r:r + 1, :]
                    kj = bs_ref[hh, 1, r:r + 1, :]
                    blocks.append(q[rows] * kj * jnp.exp2(jnp.minimum(b2[rows] - bj, 0.0)))
                xs.append(jnp.concatenate(blocks, axis=0).astype(BF16))
            scores.append((_dot(jnp.concatenate(xs, axis=1), esel_ref[...]) * diag_ref[...])[:, :c])

        for li, h in enumerate(levels):
            for hh in heads:
                q = load_q(hh)
                k = bs_ref[hh, 1]
                b2 = bs_ref[hh, 0]
                ref = jnp.concatenate(
                    [jnp.broadcast_to(bs_ref[hh, 0, m * 2 * h + h - 1:m * 2 * h + h, :], (2 * h, dk))
                     for m in range(c // (2 * h))], axis=0)
                sg = sgn_ref[li]
                z = (jnp.where(sg > 0.0, q, k) * jnp.exp2(sg * (b2 - ref))).astype(BF16)
                scores[hh] = scores[hh] + _dot_nt(z, z) * pair_ref[li]

        outs = []
        for hh in heads:
            vb = v_ref[rs, vcs[hh]].astype(BF16)
            outs.append(o_inter[hh] + _dot(scores[hh].astype(BF16), vb))
        for hh in heads:
            vb = v_ref[rs, vcs[hh]].astype(BF16)
            b2 = bs_ref[hh, 0]
            b_last = bs_ref[hh, 0, c - 1:c, :]
            kd = (bs_ref[hh, 1] * jnp.exp2(b_last - b2)).astype(BF16)
            st_ref[hh] = st_ref[hh] * jnp.exp2(b_last) + _dot_tn(vb, kd)
        for hh in heads:
            o = outs[hh]
            ms = jnp.mean(o * o, axis=-1, keepdims=True)
            y = o * lax.rsqrt(ms + RMS_EPS) * gamma_ref[:, vcs[hh]] * gate_act(gate_ref[rs, vcs[hh]])
            o_ref[rs, vcs[hh]] = y.astype(o_ref.dtype)
        return carry

    lax.fori_loop(0, n_chunks, step, 0)


def gla_recurrence(q, k, v, g, gate, gamma, *, heads, hb, dk, dv, gate_act, q_scale=1.0, offs=(0, 0, 0, 0, 0),
                   tb=512, chunk=GLA_CHUNK, name="gla"):
    t = q.shape[0]
    tb = min(tb, t)
    masks = _gla_masks(chunk, dk)
    body = functools.partial(_gla_kernel, chunk=chunk, n_chunks=tb // chunk, hb=hb, dk=dk, dv=dv,
                             q_scale=q_scale, gate_act=gate_act)
    kspec = lambda off: pl.BlockSpec((tb, hb * dk), lambda h, i: (i, h + off))
    vspec = lambda off: pl.BlockSpec((tb, hb * dv), lambda h, i: (i, h + off))
    full = lambda a: pl.BlockSpec(a.shape, lambda h, i: (0,) * a.ndim)
    return pl.pallas_call(
        body, grid=(heads // hb, t // tb),
        in_specs=[kspec(offs[0]), kspec(offs[1]), vspec(offs[2]), kspec(offs[3]), vspec(offs[4]),
                  pl.BlockSpec((1, hb * dv), lambda h, i: (0, h))] + [full(a) for a in masks],
        out_specs=vspec(0),
        out_shape=jax.ShapeDtypeStruct((t, heads * dv), BF16),
        scratch_shapes=[pltpu.VMEM((hb, dv, dk), F32), pltpu.VMEM((hb, 2, chunk, dk), F32)],
        compiler_params=_cparams("parallel", "arbitrary"), name=name,
    )(q, k, v, g, gate, gamma, *masks)


SSD_HP = SSD_PAIR * SSM_HPG
SSD_ROWS = SUBLANES


def _ssd_bcast_matrix():
    sel = np.zeros((6 * SSD_ROWS, 2 * SSD_HP * LANES), np.float32)
    for h in range(SSD_HP):
        for part in range(3):
            sel[part * SSD_ROWS + h, h * LANES:(h + 1) * LANES] = 1.0
        for part in range(2):
            sel[(3 + part) * SSD_ROWS + h, (SSD_HP + h) * LANES:(SSD_HP + h + 1) * LANES] = 1.0
    return jnp.asarray(sel, BF16)


def _ssd_kernel(x_ref, b_ref, c_ref, z_ref, dtt_ref, biast_ref, at_ref, d_ref, gamma_ref, sel_ref,
                o_ref, st_ref, rowq_ref, colb_ref, *, n_chunks):
    c = SSD_CHUNK
    p = SSM_HEADDIM
    n = SSM_STATE
    npr = SSD_HP // 2

    @pl.when(pl.program_id(1) == 0)
    def _():
        st_ref[...] = jnp.zeros(st_ref.shape, F32)

    r2 = lax.broadcasted_iota(jnp.int32, (c, c), 0)
    c2 = lax.broadcasted_iota(jnp.int32, (c, c), 1)
    causal = r2 >= c2
    low = c2 < p
    top = r2 < p
    triu = (r2 <= c2).astype(BF16)
    biast = biast_ref[...]
    a_col = at_ref[...] * LOG2E

    tb = n_chunks * c
    dtt_st = jnp.concatenate([_softplus(dtt_ref[ci] + biast) for ci in range(n_chunks)], axis=0)
    e1, e2, e3 = _split3(dtt_st * jnp.concatenate([a_col] * n_chunks, axis=0))
    pre = _dot(jnp.concatenate([e1, e2, e3], axis=0), triu)
    nr = n_chunks * SSD_ROWS
    cst_st = (pre[:nr] + pre[nr:2 * nr]) + pre[2 * nr:]
    unstack = lambda v_: jnp.concatenate([v_[ci * SSD_ROWS:(ci + 1) * SSD_ROWS] for ci in range(n_chunks)], axis=1)
    dtt_all = unstack(dtt_st)
    cst_all = unstack(cst_st)
    c1, c2_, c3 = _split3(cst_all)
    d1 = dtt_all.astype(BF16)
    d2 = (dtt_all - d1.astype(F32)).astype(BF16)
    parts = jnp.concatenate([v_.astype(F32) for v_ in (c1, c2_, c3, d1, d2)]
                            + [jnp.zeros((SSD_ROWS, tb), F32)], axis=0).astype(BF16)
    colb_all = _dot_tn(parts, sel_ref[...])
    for ci in range(n_chunks):
        rowq_ref[ci, 0] = dtt_all[:, ci * c:(ci + 1) * c]
        rowq_ref[ci, 1] = cst_all[:, ci * c:(ci + 1) * c]
        colb_ref[ci] = colb_all[ci * c:(ci + 1) * c]

    def step(ci, carry):
        rs = pl.ds(pl.multiple_of(ci * c, c), c)
        dtt = rowq_ref[ci, 0]
        cst = rowq_ref[ci, 1]
        colb = colb_ref.at[ci]
        bis = [b_ref[rs, gl * n:(gl + 1) * n].astype(BF16) for gl in range(SSD_PAIR)]
        cms = [c_ref[rs, gl * n:(gl + 1) * n].astype(BF16) for gl in range(SSD_PAIR)]
        cbs = [_dot_nt(cms[gl], bis[gl]) for gl in range(SSD_PAIR)]

        ypairs = []
        for pr in range(npr):
            lanes = slice(pr * LANES, (pr + 1) * LANES)
            h0, h1 = 2 * pr, 2 * pr + 1
            g0, g1 = h0 // SSM_HPG, h1 // SSM_HPG
            xp = x_ref[rs, lanes]
            xpb = xp.astype(BF16)
            ysel = []
            for h in (h0, h1):
                cs_b = colb[:, h * LANES:(h + 1) * LANES]
                lmat = jnp.where(causal, jnp.exp2(jnp.minimum(cs_b - cst[h:h + 1, :], 0.0)), 0.0)
                m = (cbs[h // SSM_HPG] * lmat * dtt[h:h + 1, :]).astype(BF16)
                ysel.append(_dot(m, xpb))
            y = jnp.where(low, ysel[0], ysel[1])
            csb = jnp.where(low, colb[:, h0 * LANES:(h0 + 1) * LANES], colb[:, h1 * LANES:(h1 + 1) * LANES])
            dtb = jnp.where(low, colb[:, (SSD_HP + h0) * LANES:(SSD_HP + h0 + 1) * LANES],
                            colb[:, (SSD_HP + h1) * LANES:(SSD_HP + h1 + 1) * LANES])
            st = st_ref[pr]
            stb = st.astype(BF16)
            ystate = _dot_nt(cms[g0], stb)
            if g1 != g0:
                ystate = jnp.where(low, ystate, _dot_nt(cms[g1], stb))
            y = y + ystate * jnp.exp2(csb)
            last = csb[c - 1:c, :]
            xu = (xp * (dtb * jnp.exp2(last - csb))).astype(BF16)
            upd = _dot_tn(xu, bis[g0])
            if g1 != g0:
                upd = jnp.where(top, upd, _dot_tn(xu, bis[g1]))
            dec = jnp.where(top, jnp.exp2(cst[h0:h0 + 1, c - 1:c]), jnp.exp2(cst[h1:h1 + 1, c - 1:c]))
            st_ref[pr] = st * dec + upd
            ypairs.append((y + d_ref[:, lanes] * xp) * _silu(z_ref[rs, lanes]))

        sq = [y * y for y in ypairs]
        zero = jnp.zeros((c, c), F32)
        ss0 = jnp.sum(sq[0] + jnp.where(low, sq[1], zero), axis=-1, keepdims=True)
        ss1 = jnp.sum(sq[2] + jnp.where(low, zero, sq[1]), axis=-1, keepdims=True)
        scale = 1.0 / (SSM_HPG * p)
        r0 = lax.rsqrt(ss0 * scale + RMS_EPS)
        r1 = lax.rsqrt(ss1 * scale + RMS_EPS)
        rinv = [r0, jnp.where(low, r0, r1), r1]
        for pr in range(npr):
            lanes = slice(pr * LANES, (pr + 1) * LANES)
            o_ref[rs, lanes] = (ypairs[pr] * rinv[pr] * gamma_ref[:, lanes]).astype(o_ref.dtype)
        return carry

    lax.fori_loop(0, n_chunks, step, 0)


def ssd_recurrence(xbc, z, z_off, dt_raw, dt_bias, a_log, d_skip, gamma, *, tb=512):
    assert SSD_PAIR == 2 and SSM_HPG == 3 and 2 * SSM_HEADDIM == LANES and SSD_CHUNK == LANES
    t = xbc.shape[0]
    tb = min(tb, t)
    c = SSD_CHUNK
    npair = SSM_GROUPS // SSD_PAIR
    hp = SSD_HP
    wx = hp * SSM_HEADDIM
    wn = SSD_PAIR * SSM_STATE
    padr = lambda v_: jnp.pad(v_, [(0, 0)] * (v_.ndim - 2) + [(0, SSD_ROWS - hp), (0, 0)])
    dtt_p = padr(dt_raw.reshape(t // c, c, npair, hp).transpose(2, 0, 3, 1))
    a = -jnp.exp(a_log.astype(F32))
    colp = lambda v_: padr(v_.reshape(npair, hp, 1))
    d_lanes = jnp.repeat(d_skip.astype(F32), SSM_HEADDIM).reshape(1, SSM_WIDTH)
    sel = _ssd_bcast_matrix()
    body = functools.partial(_ssd_kernel, n_chunks=tb // c)
    b_off = SSM_WIDTH // wn
    c_off = (SSM_WIDTH + SSM_GROUPS * SSM_STATE) // wn
    pspec_col = pl.BlockSpec((None, SSD_ROWS, 1), lambda g, i: (g, 0, 0))
    lane_row = pl.BlockSpec((1, wx), lambda g, i: (0, g))
    return pl.pallas_call(
        body, grid=(npair, t // tb),
        in_specs=[pl.BlockSpec((tb, wx), lambda g, i: (i, g)),
                  pl.BlockSpec((tb, wn), lambda g, i: (i, g + b_off)),
                  pl.BlockSpec((tb, wn), lambda g, i: (i, g + c_off)),
                  pl.BlockSpec((tb, wx), lambda g, i: (i, g + z_off)),
                  pl.BlockSpec((None, tb // c, SSD_ROWS, c), lambda g, i: (g, i, 0, 0)),
                  pspec_col, pspec_col, lane_row, lane_row,
                  pl.BlockSpec(sel.shape, lambda g, i: (0, 0))],
        out_specs=pl.BlockSpec((tb, wx), lambda g, i: (i, g)),
        out_shape=jax.ShapeDtypeStruct((t, SSM_WIDTH), BF16),
        scratch_shapes=[pltpu.VMEM((hp // 2, 2 * SSM_HEADDIM, SSM_STATE), F32),
                        pltpu.VMEM((tb // c, 2, SSD_ROWS, c), F32),
                        pltpu.VMEM((tb // c, c, 2 * hp * LANES), F32)],
        compiler_params=_cparams("parallel", "arbitrary"), name="ssd",
    )(xbc, xbc, xbc, z, dtt_p, colp(dt_bias), colp(a), d_lanes, gamma.reshape(1, SSM_WIDTH), sel)


def _ep_swiglu(accs, rows, tiles):
    return (_silu(accs[0]) * accs[1],)


def _ep_identity(accs, rows, tiles):
    return (accs[0],)


def _ep_scaled_silu(scale, accs, rows, tiles):
    return (_silu(accs[0]) * scale,)


def _ep_hgrn_forget(accs, rows, tiles):
    zf = accs[0]
    lb = rows[0]
    la = jnp.log(lb)
    lc = jnp.log1p(-lb) + _log_sigmoid(zf)
    log_f = jnp.maximum(la, lc) + jnp.log1p(jnp.exp(-jnp.abs(la - lc)))
    return (log_f, (1.0 - lb) * _sigmoid(-zf))


def _ep_log_alpha(accs, rows, tiles):
    return (_log_sigmoid(accs[0] + rows[0]) * (1.0 / GLA_TAU),)


def _ep_merge(accs, rows, tiles):
    return (_sigmoid(accs[0]) * accs[3] + _sigmoid(accs[1]) * accs[4] + _sigmoid(accs[2]) * accs[5],)


def _ep_ple(accs, rows, tiles):
    y = tiles[0] + _sigmoid(accs[0]) * accs[1]
    return (y, y)


def _regroup_w_in(w_in):
    o = IN_OFFSETS
    w = w_in.astype(BF16)
    small = jnp.concatenate([w[..., o[6]:o[7]], w[..., o[11]:o[12]]], axis=-1)
    small = jnp.pad(small, ((0, 0), (0, 0), (0, LANES - small.shape[-1])))
    return w, w[..., o[7]:o[11]], w[..., o[12]:], small


def _ffn(h, hb, w_in, w_out, lead, g, b):
    mid = fused_matmul([hb], [W(w_in, 0, 0, lead), W(w_in, 0, D_FF // 512, lead)], _ep_swiglu, [BF16], D_FF,
                       name="ffn_in")[0]
    return ln_matmul(mid, W(w_out, 0, 0, lead), h, g[None, :], b[None, :], 0.5)


def _mixer(hb, layer, lb, w_all, conv_w, conv_b, dt_bias, a_log, d_skip, gate_w, gla_b_gate, mix_norm_g, w_branch):
    lead = (layer,)
    tn = 512
    w_a, w_b, w_c, w_s = w_all
    wcol = lambda col: W(w_a, 0, col // tn, lead)
    g_hg, g_ssm, g_gla = mix_norm_g[:HG_WIDTH], mix_norm_g[HG_WIDTH:HG_WIDTH + SSM_WIDTH], \
        mix_norm_g[HG_WIDTH + SSM_WIDTH:]

    hg_q = fused_matmul([hb], [wcol(COL_HG_Q)], functools.partial(_ep_scaled_silu, HG_DK ** -0.5), [F32],
                        HG_WIDTH, name="proj_hgq")[0]
    hg_logf, hg_k = fused_matmul([hb], [wcol(COL_HG_F)], _ep_hgrn_forget, [F32, F32], HG_WIDTH,
                                 rows=[lb[None, :]], name="proj_hgf")
    id1 = fused_matmul([hb], [wcol(COL_ID1)], _ep_identity, [F32], ID1_WIDTH, name="proj_id1")[0]
    xbc = conv_matmul(hb, wcol(COL_XBC), conv_w, conv_b[None, :], SSM_XBC)
    small = fused_matmul([hb], [W(w_s, 0, 0, lead)], _ep_identity, [F32], LANES, tn=LANES, name="proj_small")[0]
    id2 = fused_matmul([hb], [W(w_b, 0, 0, lead)], _ep_identity, [F32], ID2_WIDTH, name="proj_id2")[0]
    log_alpha = fused_matmul([small], [W(gate_w, 0, 0, lead)], _ep_log_alpha, [F32], GLA_HEADS * GLA_DK,
                             rows=[gla_b_gate[None, :]], name="gla_gate")[0]

    hb_hg = 6
    y_hg = gla_recurrence(hg_q, hg_k, id1, hg_logf, id1, g_hg[None, :], heads=HG_HEADS, hb=hb_hg, dk=HG_DK,
                          dv=HG_DK, gate_act=_sigmoid, offs=(0, 0, 0, 0, HG_WIDTH // (hb_hg * HG_DK)),
                          name="hgrn")
    y_ssm = ssd_recurrence(xbc, id1, 2 * HG_WIDTH // (SSD_PAIR * SSM_HPG * SSM_HEADDIM), small[:, :SSM_HEADS],
                           dt_bias, a_log, d_skip, g_ssm)
    hb_gla = 4
    wk, wv = hb_gla * GLA_DK, hb_gla * GLA_DV
    y_gla = gla_recurrence(id2, id2, id2, log_alpha, id2, g_gla[None, :], heads=GLA_HEADS, hb=hb_gla, dk=GLA_DK,
                           dv=GLA_DV, gate_act=_silu, q_scale=GLA_DK ** -0.5,
                           offs=(0, GLA_HEADS * GLA_DK // wk, 2 * GLA_HEADS * GLA_DK // wv, 0,
                                 (2 * GLA_HEADS * GLA_DK + GLA_WIDTH) // wv), name="gla")

    tn_m = 256
    nb = D_MODEL // tn_m
    merged = fused_matmul(
        [hb, y_hg, y_ssm, y_gla],
        [W(w_c, 0, 0, lead), W(w_c, 0, nb, lead), W(w_c, 0, 2 * nb, lead),
         W(w_branch, 1, 0, lead, HG_WIDTH, 0), W(w_branch, 2, 0, lead, SSM_WIDTH, 1),
         W(w_branch, 3, 0, lead, GLA_WIDTH, (HG_WIDTH + SSM_WIDTH) // GLA_WIDTH)],
        _ep_merge, [BF16], D_MODEL, tm=512, tn=tn_m, name="merge")[0]
    return merged


def kernel(x, p, ffn_w_in, ffn_w_out, ln_g, ln_b, w_in, hg_lb_logits, ssm_conv_w, ssm_conv_b, ssm_dt_bias,
           ssm_a_log, ssm_d, gla_w_gate, gla_b_gate, mix_norm_g, w_branch, w_out, ple_w_proj, ple_w_gate):
    bsz, t, d = x.shape
    lb_all = lower_bounds(hg_lb_logits.astype(F32))
    ffn_w_in_b = ffn_w_in.astype(BF16)
    ffn_w_out_b = ffn_w_out.astype(BF16)
    w_all = _regroup_w_in(w_in)
    w_branch_b = w_branch.astype(BF16)
    w_out_b = w_out.astype(BF16)
    ple_w_gate_b = ple_w_gate.astype(BF16)
    ple_w_proj_b = ple_w_proj.astype(BF16)
    gate_w = jnp.zeros((DEPTH, LANES, GLA_HEADS * GLA_DK), F32).at[:, SSM_HEADS:SSM_HEADS + GLA_RANK].set(
        gla_w_gate).astype(BF16)
    outs = []
    for bi in range(bsz):
        h = x[bi]
        hb = h.astype(BF16)
        for i in range(DEPTH):
            h, hb = _ffn(h, hb, ffn_w_in_b, ffn_w_out_b, (i, 0), ln_g[i, 0], ln_b[i, 0])
            merged = _mixer(hb, i, lb_all[i], w_all, ssm_conv_w[i], ssm_conv_b[i], ssm_dt_bias[i], ssm_a_log[i],
                            ssm_d[i], gate_w, gla_b_gate[i], mix_norm_g[i], w_branch_b)
            h, hb = ln_matmul(merged, W(w_out_b, 0, 0, (i,)), h, ln_g[i, 1][None, :], ln_b[i, 1][None, :], 1.0)
            h, hb = _ffn(h, hb, ffn_w_in_b, ffn_w_out_b, (i, 1), ln_g[i, 2], ln_b[i, 2])
            h, hb = fused_matmul([hb, p[i, bi]], [W(ple_w_gate_b, 0, 0, (i,)), W(ple_w_proj_b, 1, 0, (i,))],
                                 _ep_ple, [F32, BF16], D_MODEL, tiles=[h], name="ple")
        outs.append(h)
    return jnp.stack(outs, axis=0)
```

```python
import collections
import functools

import jax
import jax.numpy as jnp
import numpy as np
from jax import lax
from jax.experimental import pallas as pl
from jax.experimental.pallas import tpu as pltpu

F32 = jnp.float32
BF16 = jnp.bfloat16

D_MODEL = 4096
DEPTH = 4
P_DIM = 256
D_FF = 5 * D_MODEL // 8
HG_WIDTH = 1536
HG_DK = 128
HG_HEADS = 12
SSM_WIDTH = 1536
SSM_HEADDIM = 64
SSM_HEADS = 24
SSM_GROUPS = 8
SSM_HPG = 3
SSM_STATE = 128
SSM_CONV = 4
SSM_XBC = SSM_WIDTH + 2 * SSM_GROUPS * SSM_STATE
GLA_WIDTH = 1024
GLA_HEADS = 4
GLA_DV = 256
GLA_DK = 128
GLA_RANK = 16
GLA_TAU = 16.0
IN_SPLITS = (HG_WIDTH, HG_WIDTH, HG_WIDTH, HG_WIDTH, SSM_WIDTH, SSM_XBC, SSM_HEADS,
             GLA_HEADS * GLA_DK, GLA_HEADS * GLA_DK, GLA_WIDTH, GLA_WIDTH, GLA_RANK, 3 * D_MODEL)
IN_OFFSETS = tuple(int(o) for o in np.cumsum((0,) + IN_SPLITS))
ALPHA = float((2 * DEPTH) ** 0.25)
LN_EPS = 1e-5
RMS_EPS = 1e-6
LOG2E = float(np.log2(np.e))

VMEM_LIMIT_BYTES = 56 * 1024 * 1024
LANES = 128
SUBLANES = 8

GLA_CHUNK = 64
SSD_CHUNK = 128
SSD_PAIR = 2

COL_HG_Q = 0
COL_HG_F = HG_WIDTH
COL_ID1 = 2 * HG_WIDTH
COL_XBC = COL_ID1 + 2 * HG_WIDTH + SSM_WIDTH
ID1_WIDTH = 2 * HG_WIDTH + SSM_WIDTH
ID2_WIDTH = 2 * GLA_HEADS * GLA_DK + 2 * GLA_WIDTH

W = collections.namedtuple("W", "w x col lead krows kblk", defaults=(0, (), None, 0))


def _cparams(*sem):
    return pltpu.CompilerParams(dimension_semantics=sem, vmem_limit_bytes=VMEM_LIMIT_BYTES)


def _sigmoid(x):
    return 1.0 / (1.0 + jnp.exp(-x))


def _silu(x):
    return x * _sigmoid(x)


def _log_sigmoid(x):
    return jnp.minimum(x, 0.0) - jnp.log1p(jnp.exp(-jnp.abs(x)))


def _softplus(x):
    return jnp.maximum(x, 0.0) + jnp.log1p(jnp.exp(-jnp.abs(x)))


def _dot(a, b):
    return jnp.dot(a, b, preferred_element_type=F32)


def _dot_nt(a, b):
    return lax.dot_general(a, b, (((1,), (1,)), ((), ())), preferred_element_type=F32)


def _dot_tn(a, b):
    return lax.dot_general(a, b, (((0,), (0,)), ((), ())), preferred_element_type=F32)


def _split3(x):
    x1 = x.astype(BF16)
    r1 = x - x1.astype(F32)
    x2 = r1.astype(BF16)
    x3 = (r1 - x2.astype(F32)).astype(BF16)
    return x1, x2, x3


def _prefix_rows(tril_bf16, x):
    x1, x2, x3 = _split3(x)
    return (_dot(tril_bf16, x1) + _dot(tril_bf16, x2)) + _dot(tril_bf16, x3)


def _prefix_lanes(x, triu_bf16):
    x1, x2, x3 = _split3(x)
    return (_dot(x1, triu_bf16) + _dot(x2, triu_bf16)) + _dot(x3, triu_bf16)


def _wspec(w, tn, nargs=2):
    lead = tuple(w.lead)
    krows = w.w.shape[-2] if w.krows is None else w.krows
    return pl.BlockSpec((None,) * len(lead) + (krows, tn),
                        lambda i, j: lead + (w.kblk, j + w.col))


def _mm_kernel(*refs, n_x, w_x, n_rows, n_tiles, n_out, epilogue, tm, sub_m):
    x_refs = refs[:n_x]
    w_refs = refs[n_x:n_x + len(w_x)]
    pos = n_x + len(w_x)
    row_refs = refs[pos:pos + n_rows]
    pos += n_rows
    tile_refs = refs[pos:pos + n_tiles]
    pos += n_tiles
    out_refs = refs[pos:pos + n_out]

    wvals = [None if w.dtype == BF16 else w[...].astype(BF16) for w in w_refs]

    def sub(r, carry):
        rs = pl.ds(pl.multiple_of(r * sub_m, sub_m), sub_m)
        xs = [x[rs, :].astype(BF16) for x in x_refs]
        accs = [_dot(xs[xi], w[...] if wv is None else wv) for w, wv, xi in zip(w_refs, wvals, w_x)]
        outs = epilogue(accs, [rr[...] for rr in row_refs], [t[rs, :] for t in tile_refs])
        for o_ref, o in zip(out_refs, outs):
            o_ref[rs, :] = o.astype(o_ref.dtype)
        return carry

    lax.fori_loop(0, tm // sub_m, sub, 0, unroll=True)


def fused_matmul(xs, ws, epilogue, out_dtypes, n_cols, *, rows=(), tiles=(), tm=1024, tn=512,
                 sub_m=256, name="mm"):
    m = xs[0].shape[0]
    tm = min(tm, m)
    sub_m = min(sub_m, tm)
    grid = (m // tm, n_cols // tn)
    in_specs = [pl.BlockSpec((tm, x.shape[1]), lambda i, j: (i, 0)) for x in xs]
    in_specs += [_wspec(w, tn) for w in ws]
    in_specs += [pl.BlockSpec((1, tn), lambda i, j: (0, j)) for _ in rows]
    in_specs += [pl.BlockSpec((tm, tn), lambda i, j: (i, j)) for _ in tiles]
    out_specs = [pl.BlockSpec((tm, tn), lambda i, j: (i, j)) for _ in out_dtypes]
    out_shape = [jax.ShapeDtypeStruct((m, n_cols), dt) for dt in out_dtypes]
    body = functools.partial(_mm_kernel, n_x=len(xs), w_x=tuple(w.x for w in ws), n_rows=len(rows),
                             n_tiles=len(tiles), n_out=len(out_dtypes), epilogue=epilogue, tm=tm, sub_m=sub_m)
    return pl.pallas_call(
        body, grid=grid, in_specs=in_specs, out_specs=out_specs, out_shape=out_shape,
        compiler_params=_cparams("parallel", "arbitrary"), name=name,
    )(*xs, *[w.w for w in ws], *rows, *tiles)


def _conv_mm_kernel(x_ref, w_ref, cw_ref, cb_ref, o_ref, carry_ref, *, tm, tn, sub_m):
    i = pl.program_id(0)
    j = pl.program_id(1)
    cols = pl.ds(pl.multiple_of(j * tn, tn), tn)

    @pl.when(i == 0)
    def _():
        carry_ref[:, cols] = jnp.zeros((SUBLANES, tn), F32)

    prev = carry_ref[:, cols]
    wv = w_ref[...].astype(BF16)
    for r in range(tm // sub_m):
        lo = r * sub_m
        z = _dot(x_ref[lo:lo + sub_m, :], wv)
        zext = jnp.concatenate([prev, z], axis=0)
        acc = z * cw_ref[SSM_CONV - 1:SSM_CONV, :] + cb_ref[...]
        for back in range(1, SSM_CONV):
            k = SSM_CONV - 1 - back
            acc = acc + pltpu.roll(zext, back, 0)[SUBLANES:, :] * cw_ref[k:k + 1, :]
        o_ref[lo:lo + sub_m, :] = _silu(acc)
        prev = z[sub_m - SUBLANES:, :]
    carry_ref[:, cols] = prev


def conv_matmul(x, w, conv_w, conv_b, n, *, tm=1024, tn=512, sub_m=256):
    m, k = x.shape
    tm = min(tm, m)
    sub_m = min(sub_m, tm)
    body = functools.partial(_conv_mm_kernel, tm=tm, tn=tn, sub_m=sub_m)
    return pl.pallas_call(
        body, grid=(m // tm, n // tn),
        in_specs=[pl.BlockSpec((tm, k), lambda i, j: (i, 0)),
                  _wspec(w, tn),
                  pl.BlockSpec((SSM_CONV, tn), lambda i, j: (0, j)),
                  pl.BlockSpec((1, tn), lambda i, j: (0, j))],
        out_specs=pl.BlockSpec((tm, tn), lambda i, j: (i, j)),
        out_shape=jax.ShapeDtypeStruct((m, n), F32),
        scratch_shapes=[pltpu.VMEM((SUBLANES, n), F32)],
        compiler_params=_cparams("arbitrary", "arbitrary"), name="conv_mm",
    )(x, w.w, conv_w, conv_b)


def _ln_kernel(z_ref, g_ref, b_ref, o_ref, obf_ref, *, tr, ln_m):
    def norm(r, carry):
        rs = pl.ds(pl.multiple_of(r * ln_m, ln_m), ln_m)
        z = z_ref[rs, :]
        mu = jnp.mean(z, axis=-1, keepdims=True)
        zc = z - mu
        var = jnp.mean(zc * zc, axis=-1, keepdims=True)
        y = zc * lax.rsqrt(var + LN_EPS) * g_ref[...] + b_ref[...]
        o_ref[rs, :] = y
        obf_ref[rs, :] = y.astype(BF16)
        return carry

    lax.fori_loop(0, tr // ln_m, norm, 0, unroll=4)


def layer_norm_rows(z, gamma, beta, *, tr=256, ln_m=8):
    m, n = z.shape
    tr = min(tr, m)
    row = pl.BlockSpec((tr, n), lambda i: (i, 0))
    vec = pl.BlockSpec((1, n), lambda i: (0, 0))
    return pl.pallas_call(
        functools.partial(_ln_kernel, tr=tr, ln_m=ln_m), grid=(m // tr,),
        in_specs=[row, vec, vec], out_specs=[row, row],
        out_shape=[jax.ShapeDtypeStruct((m, n), F32), jax.ShapeDtypeStruct((m, n), BF16)],
        compiler_params=_cparams("parallel"), name="ln_rows",
    )(z, gamma, beta)


def _ep_resid(scale, accs, rows, tiles):
    return (ALPHA * tiles[0] + scale * accs[0],)


def ln_matmul(x, w, resid, gamma, beta, scale):
    z = fused_matmul([x], [w], functools.partial(_ep_resid, scale), [F32], resid.shape[1], tiles=[resid],
                     name="resid_mm")[0]
    return layer_norm_rows(z, gamma, beta)


def _lb_kernel(x_ref, lb_ref):
    x = x_ref[...]
    mx = jnp.max(x, axis=0, keepdims=True)
    e = jnp.exp(x - mx)
    p = e / jnp.sum(e, axis=0, keepdims=True)
    c = p[0:1, :]
    first = c
    lb_ref[0:1, :] = c - first
    for i in range(1, DEPTH):
        c = c + p[i:i + 1, :]
        lb_ref[i:i + 1, :] = c - first


def lower_bounds(logits):
    return pl.pallas_call(_lb_kernel, out_shape=jax.ShapeDtypeStruct(logits.shape, F32), name="hgrn_lb")(logits)


def _gla_levels(chunk):
    hs = []
    h = SUBLANES
    while h < chunk:
        hs.append(h)
        h *= 2
    return hs


def _gla_masks(chunk, dk):
    c = chunk
    t = np.arange(c)
    tril = (t[:, None] >= t[None, :]).astype(np.float32)
    pair, sgn = [], []
    for h in _gla_levels(c):
        is_q = (t % (2 * h)) >= h
        same = (t[:, None] // (2 * h)) == (t[None, :] // (2 * h))
        pair.append((same & is_q[:, None] & ~is_q[None, :]).astype(np.float32))
        sgn.append(np.broadcast_to(np.where(is_q, 1.0, -1.0)[:, None], (c, dk)).astype(np.float32))
    lane = np.arange(LANES)
    esel = np.tile(((lane[None, :] % SUBLANES) == np.arange(SUBLANES)[:, None]).astype(np.float32)[:, None, :],
                   (1, dk, 1)).reshape(SUBLANES * dk, LANES)
    diag = ((lane[None, :] < c) & (lane[None, :] // SUBLANES == t[:, None] // SUBLANES)
            & (t[:, None] % SUBLANES >= lane[None, :] % SUBLANES)).astype(np.float32)
    return (jnp.asarray(tril, BF16), jnp.asarray(np.stack(pair)), jnp.asarray(np.stack(sgn)),
            jnp.asarray(esel, BF16), jnp.asarray(diag))


def _gla_kernel(q_ref, k_ref, v_ref, g_ref, gate_ref, gamma_ref, tril_ref, pair_ref, sgn_ref, esel_ref, diag_ref,
                o_ref, st_ref, bs_ref, sc_ref, *, chunk, n_chunks, hb, dk, dv, q_scale, gate_act):
    c = chunk
    nblk = c // SUBLANES
    levels = _gla_levels(c)

    @pl.when(pl.program_id(1) == 0)
    def _():
        st_ref[...] = jnp.zeros(st_ref.shape, F32)

    heads = range(hb)
    kcs = [slice(hh * dk, (hh + 1) * dk) for hh in heads]
    vcs = [slice(hh * dv, (hh + 1) * dv) for hh in heads]

    def rows_of(ci):
        return pl.ds(ci * c if isinstance(ci, int) else pl.multiple_of(ci * c, c), c)

    def load_q(rs, hh):
        q = q_ref[rs, kcs[hh]]
        return q * q_scale if q_scale != 1.0 else q

    def front_prefix(ci, s):
        rs = rows_of(ci)
        tril = tril_ref[...]
        for hh in heads:
            bs_ref[s, hh, 0] = _prefix_rows(tril, g_ref[rs, kcs[hh]]) * LOG2E
            bs_ref[s, hh, 1] = k_ref[rs, kcs[hh]]

    def front_diag(ci, s):
        rs = rows_of(ci)
        scores = []
        for hh in heads:
            q = load_q(rs, hh)
            b2 = bs_ref[s, hh, 0]
            xs = []
            for j in range(SUBLANES):
                blocks = []
                for p in range(nblk):
                    r = p * SUBLANES + j
                    rows = slice(p * SUBLANES, (p + 1) * SUBLANES)
                    bj = bs_ref[s, hh, 0, r:r + 1, :]
                    kj = bs_ref[s, hh, 1, r:r + 1, :]
                    blocks.append(q[rows] * kj * jnp.exp2(jnp.minimum(b2[rows] - bj, 0.0)))
                xs.append(jnp.concatenate(blocks, axis=0).astype(BF16))
            scores.append((_dot(jnp.concatenate(xs, axis=1), esel_ref[...]) * diag_ref[...])[:, :c])
        return scores

    def front_tree(ci, s, scores):
        rs = rows_of(ci)
        for li, h in enumerate(levels):
            for hh in heads:
                q = load_q(rs, hh)
                k = bs_ref[s, hh, 1]
                b2 = bs_ref[s, hh, 0]
                ref = jnp.concatenate(
                    [jnp.broadcast_to(bs_ref[s, hh, 0, m * 2 * h + h - 1:m * 2 * h + h, :], (2 * h, dk))
                     for m in range(c // (2 * h))], axis=0)
                sg = sgn_ref[li]
                z = (jnp.where(sg > 0.0, q, k) * jnp.exp2(sg * (b2 - ref))).astype(BF16)
                scores[hh] = scores[hh] + _dot_nt(z, z) * pair_ref[li]
        for hh in heads:
            sc_ref[s, hh] = scores[hh].astype(BF16)

    def back_state(ci, s):
        rs = rows_of(ci)
        return [_dot_nt((load_q(rs, hh) * jnp.exp2(bs_ref[s, hh, 0])).astype(BF16), st_ref[hh].astype(BF16))
                for hh in heads]

    def back_out(ci, s, o_inter):
        rs = rows_of(ci)
        return [o_inter[hh] + _dot(sc_ref[s, hh], v_ref[rs, vcs[hh]].astype(BF16)) for hh in heads]

    def back_update(ci, s, outs):
        rs = rows_of(ci)
        for hh in heads:
            vb = v_ref[rs, vcs[hh]].astype(BF16)
            b_last = bs_ref[s, hh, 0, c - 1:c, :]
            kd = (bs_ref[s, hh, 1] * jnp.exp2(b_last - bs_ref[s, hh, 0])).astype(BF16)
            st_ref[hh] = st_ref[hh] * jnp.exp2(b_last) + _dot_tn(vb, kd)
        for hh in heads:
            o = outs[hh]
            ms = jnp.mean(o * o, axis=-1, keepdims=True)
            y = o * lax.rsqrt(ms + RMS_EPS) * gamma_ref[:, vcs[hh]] * gate_act(gate_ref[rs, vcs[hh]])
            o_ref[rs, vcs[hh]] = y.astype(o_ref.dtype)

    def overlapped(cb, sb, cf, sf):
        o_inter = back_state(cb, sb)
        front_prefix(cf, sf)
        outs = back_out(cb, sb, o_inter)
        scores = front_diag(cf, sf)
        back_update(cb, sb, outs)
        front_tree(cf, sf, scores)

    front_prefix(0, 0)
    front_tree(0, 0, front_diag(0, 0))

    def step(i, carry):
        overlapped(2 * i, 0, 2 * i + 1, 1)
        overlapped(2 * i + 1, 1, jnp.minimum(2 * i + 2, n_chunks - 1), 0)
        return carry

    lax.fori_loop(0, n_chunks // 2, step, 0)


def gla_recurrence(q, k, v, g, gate, gamma, *, heads, hb, dk, dv, gate_act, q_scale=1.0, offs=(0, 0, 0, 0, 0),
                   tb=1024, chunk=GLA_CHUNK, name="gla"):
    t = q.shape[0]
    tb = min(tb, t)
    masks = _gla_masks(chunk, dk)
    body = functools.partial(_gla_kernel, chunk=chunk, n_chunks=tb // chunk, hb=hb, dk=dk, dv=dv,
                             q_scale=q_scale, gate_act=gate_act)
    kspec = lambda off: pl.BlockSpec((tb, hb * dk), lambda h, i: (i, h + off))
    vspec = lambda off: pl.BlockSpec((tb, hb * dv), lambda h, i: (i, h + off))
    full = lambda a: pl.BlockSpec(a.shape, lambda h, i: (0,) * a.ndim)
    return pl.pallas_call(
        body, grid=(heads // hb, t // tb),
        in_specs=[kspec(offs[0]), kspec(offs[1]), vspec(offs[2]), kspec(offs[3]), vspec(offs[4]),
                  pl.BlockSpec((1, hb * dv), lambda h, i: (0, h))] + [full(a) for a in masks],
        out_specs=vspec(0),
        out_shape=jax.ShapeDtypeStruct((t, heads * dv), BF16),
        scratch_shapes=[pltpu.VMEM((hb, dv, dk), F32), pltpu.VMEM((2, hb, 2, chunk, dk), F32),
                        pltpu.VMEM((2, hb, chunk, chunk), BF16)],
        compiler_params=_cparams("parallel", "arbitrary"), name=name,
    )(q, k, v, g, gate, gamma, *masks)


SSD_HP = SSD_PAIR * SSM_HPG
SSD_ROWS = SUBLANES


def _ssd_bcast_matrix():
    sel = np.zeros((6 * SSD_ROWS, 2 * SSD_HP * LANES), np.float32)
    for h in range(SSD_HP):
        for part in range(3):
            sel[part * SSD_ROWS + h, h * LANES:(h + 1) * LANES] = 1.0
        for part in range(2):
            sel[(3 + part) * SSD_ROWS + h, (SSD_HP + h) * LANES:(SSD_HP + h + 1) * LANES] = 1.0
    return jnp.asarray(sel, BF16)


def _ssd_kernel(x_ref, b_ref, c_ref, z_ref, dtt_ref, biast_ref, at_ref, d_ref, gamma_ref, sel_ref,
                o_ref, st_ref, rowq_ref, colb_ref, *, n_chunks):
    c = SSD_CHUNK
    p = SSM_HEADDIM
    n = SSM_STATE
    npr = SSD_HP // 2

    @pl.when(pl.program_id(1) == 0)
    def _():
        st_ref[...] = jnp.zeros(st_ref.shape, F32)

    r2 = lax.broadcasted_iota(jnp.int32, (c, c), 0)
    c2 = lax.broadcasted_iota(jnp.int32, (c, c), 1)
    causal = r2 >= c2
    low = c2 < p
    top = r2 < p
    triu = (r2 <= c2).astype(BF16)
    biast = biast_ref[...]
    a_col = at_ref[...] * LOG2E

    tb = n_chunks * c
    dtt_st = jnp.concatenate([_softplus(dtt_ref[ci] + biast) for ci in range(n_chunks)], axis=0)
    e1, e2, e3 = _split3(dtt_st * jnp.concatenate([a_col] * n_chunks, axis=0))
    pre = _dot(jnp.concatenate([e1, e2, e3], axis=0), triu)
    nr = n_chunks * SSD_ROWS
    cst_st = (pre[:nr] + pre[nr:2 * nr]) + pre[2 * nr:]
    unstack = lambda v_: jnp.concatenate([v_[ci * SSD_ROWS:(ci + 1) * SSD_ROWS] for ci in range(n_chunks)], axis=1)
    dtt_all = unstack(dtt_st)
    cst_all = unstack(cst_st)
    c1, c2_, c3 = _split3(cst_all)
    d1 = dtt_all.astype(BF16)
    d2 = (dtt_all - d1.astype(F32)).astype(BF16)
    parts = jnp.concatenate([v_.astype(F32) for v_ in (c1, c2_, c3, d1, d2)]
                            + [jnp.zeros((SSD_ROWS, tb), F32)], axis=0).astype(BF16)
    colb_all = _dot_tn(parts, sel_ref[...])
    for ci in range(n_chunks):
        rowq_ref[ci, 0] = dtt_all[:, ci * c:(ci + 1) * c]
        rowq_ref[ci, 1] = cst_all[:, ci * c:(ci + 1) * c]
        colb_ref[ci] = colb_all[ci * c:(ci + 1) * c]

    def step(ci, carry):
        rs = pl.ds(pl.multiple_of(ci * c, c), c)
        dtt = rowq_ref[ci, 0]
        cst = rowq_ref[ci, 1]
        colb = colb_ref.at[ci]
        bis = [b_ref[rs, gl * n:(gl + 1) * n].astype(BF16) for gl in range(SSD_PAIR)]
        cms = [c_ref[rs, gl * n:(gl + 1) * n].astype(BF16) for gl in range(SSD_PAIR)]
        cbs = [_dot_nt(cms[gl], bis[gl]) for gl in range(SSD_PAIR)]

        ypairs = []
        for pr in range(npr):
            lanes = slice(pr * LANES, (pr + 1) * LANES)
            h0, h1 = 2 * pr, 2 * pr + 1
            g0, g1 = h0 // SSM_HPG, h1 // SSM_HPG
            xp = x_ref[rs, lanes]
            xpb = xp.astype(BF16)
            ysel = []
            for h in (h0, h1):
                cs_b = colb[:, h * LANES:(h + 1) * LANES]
                lmat = jnp.where(causal, jnp.exp2(jnp.minimum(cs_b - cst[h:h + 1, :], 0.0)), 0.0)
                m = (cbs[h // SSM_HPG] * lmat * dtt[h:h + 1, :]).astype(BF16)
                ysel.append(_dot(m, xpb))
            y = jnp.where(low, ysel[0], ysel[1])
            csb = jnp.where(low, colb[:, h0 * LANES:(h0 + 1) * LANES], colb[:, h1 * LANES:(h1 + 1) * LANES])
            dtb = jnp.where(low, colb[:, (SSD_HP + h0) * LANES:(SSD_HP + h0 + 1) * LANES],
                            colb[:, (SSD_HP + h1) * LANES:(SSD_HP + h1 + 1) * LANES])
            st = st_ref[pr]
            stb = st.astype(BF16)
            ystate = _dot_nt(cms[g0], stb)
            if g1 != g0:
                ystate = jnp.where(low, ystate, _dot_nt(cms[g1], stb))
            y = y + ystate * jnp.exp2(csb)
            last = csb[c - 1:c, :]
            xu = (xp * (dtb * jnp.exp2(last - csb))).astype(BF16)
            upd = _dot_tn(xu, bis[g0])
            if g1 != g0:
                upd = jnp.where(top, upd, _dot_tn(xu, bis[g1]))
            dec = jnp.where(top, jnp.exp2(cst[h0:h0 + 1, c - 1:c]), jnp.exp2(cst[h1:h1 + 1, c - 1:c]))
            st_ref[pr] = st * dec + upd
            ypairs.append((y + d_ref[:, lanes] * xp) * _silu(z_ref[rs, lanes]))

        sq = [y * y for y in ypairs]
        zero = jnp.zeros((c, c), F32)
        ss0 = jnp.sum(sq[0] + jnp.where(low, sq[1], zero), axis=-1, keepdims=True)
        ss1 = jnp.sum(sq[2] + jnp.where(low, zero, sq[1]), axis=-1, keepdims=True)
        scale = 1.0 / (SSM_HPG * p)
        r0 = lax.rsqrt(ss0 * scale + RMS_EPS)
        r1 = lax.rsqrt(ss1 * scale + RMS_EPS)
        rinv = [r0, jnp.where(low, r0, r1), r1]
        for pr in range(npr):
            lanes = slice(pr * LANES, (pr + 1) * LANES)
            o_ref[rs, lanes] = (ypairs[pr] * rinv[pr] * gamma_ref[:, lanes]).astype(o_ref.dtype)
        return carry

    lax.fori_loop(0, n_chunks, step, 0)


def ssd_recurrence(xbc, z, z_off, dt_raw, dt_bias, a_log, d_skip, gamma, *, tb=512):
    assert SSD_PAIR == 2 and SSM_HPG == 3 and 2 * SSM_HEADDIM == LANES and SSD_CHUNK == LANES
    t = xbc.shape[0]
    tb = min(tb, t)
    c = SSD_CHUNK
    npair = SSM_GROUPS // SSD_PAIR
    hp = SSD_HP
    wx = hp * SSM_HEADDIM
    wn = SSD_PAIR * SSM_STATE
    padr = lambda v_: jnp.pad(v_, [(0, 0)] * (v_.ndim - 2) + [(0, SSD_ROWS - hp), (0, 0)])
    dtt_p = padr(dt_raw.reshape(t // c, c, npair, hp).transpose(2, 0, 3, 1))
    a = -jnp.exp(a_log.astype(F32))
    colp = lambda v_: padr(v_.reshape(npair, hp, 1))
    d_lanes = jnp.repeat(d_skip.astype(F32), SSM_HEADDIM).reshape(1, SSM_WIDTH)
    sel = _ssd_bcast_matrix()
    body = functools.partial(_ssd_kernel, n_chunks=tb // c)
    b_off = SSM_WIDTH // wn
    c_off = (SSM_WIDTH + SSM_GROUPS * SSM_STATE) // wn
    pspec_col = pl.BlockSpec((None, SSD_ROWS, 1), lambda g, i: (g, 0, 0))
    lane_row = pl.BlockSpec((1, wx), lambda g, i: (0, g))
    return pl.pallas_call(
        body, grid=(npair, t // tb),
        in_specs=[pl.BlockSpec((tb, wx), lambda g, i: (i, g)),
                  pl.BlockSpec((tb, wn), lambda g, i: (i, g + b_off)),
                  pl.BlockSpec((tb, wn), lambda g, i: (i, g + c_off)),
                  pl.BlockSpec((tb, wx), lambda g, i: (i, g + z_off)),
                  pl.BlockSpec((None, tb // c, SSD_ROWS, c), lambda g, i: (g, i, 0, 0)),
                  pspec_col, pspec_col, lane_row, lane_row,
                  pl.BlockSpec(sel.shape, lambda g, i: (0, 0))],
        out_specs=pl.BlockSpec((tb, wx), lambda g, i: (i, g)),
        out_shape=jax.ShapeDtypeStruct((t, SSM_WIDTH), BF16),
        scratch_shapes=[pltpu.VMEM((hp // 2, 2 * SSM_HEADDIM, SSM_STATE), F32),
                        pltpu.VMEM((tb // c, 2, SSD_ROWS, c), F32),
                        pltpu.VMEM((tb // c, c, 2 * hp * LANES), F32)],
        compiler_params=_cparams("parallel", "arbitrary"), name="ssd",
    )(xbc, xbc, xbc, z, dtt_p, colp(dt_bias), colp(a), d_lanes, gamma.reshape(1, SSM_WIDTH), sel)


def _ep_swiglu(accs, rows, tiles):
    return (_silu(accs[0]) * accs[1],)


def _ep_identity(accs, rows, tiles):
    return (accs[0],)


def _ep_scaled_silu(scale, accs, rows, tiles):
    return (_silu(accs[0]) * scale,)


def _ep_hgrn_forget(accs, rows, tiles):
    zf = accs[0]
    lb = rows[0]
    la = jnp.log(lb)
    lc = jnp.log1p(-lb) + _log_sigmoid(zf)
    log_f = jnp.maximum(la, lc) + jnp.log1p(jnp.exp(-jnp.abs(la - lc)))
    return (log_f, (1.0 - lb) * _sigmoid(-zf))


def _ep_log_alpha(accs, rows, tiles):
    return (_log_sigmoid(accs[0] + rows[0]) * (1.0 / GLA_TAU),)


def _ep_merge(accs, rows, tiles):
    return (_sigmoid(accs[0]) * accs[3] + _sigmoid(accs[1]) * accs[4] + _sigmoid(accs[2]) * accs[5],)


def _ep_ple(accs, rows, tiles):
    y = tiles[0] + _sigmoid(accs[0]) * accs[1]
    return (y, y)


def _regroup_w_in(w_in):
    o = IN_OFFSETS
    small = jnp.concatenate([w_in[..., o[6]:o[7]], w_in[..., o[11]:o[12]]], axis=-1)
    small = jnp.pad(small, ((0, 0), (0, 0), (0, LANES - small.shape[-1])))
    return w_in, w_in[..., o[7]:o[11]].astype(BF16), w_in[..., o[12]:].astype(BF16), small.astype(BF16)


def _ffn(h, hb, w_in, w_out, lead, g, b):
    mid = fused_matmul([hb], [W(w_in, 0, 0, lead), W(w_in, 0, D_FF // 512, lead)], _ep_swiglu, [BF16], D_FF,
                       name="ffn_in")[0]
    return ln_matmul(mid, W(w_out, 0, 0, lead), h, g[None, :], b[None, :], 0.5)


def _mixer(hb, layer, lb, w_all, conv_w, conv_b, dt_bias, a_log, d_skip, gate_w, gla_b_gate, mix_norm_g, w_branch):
    lead = (layer,)
    tn = 512
    w_a, w_b, w_c, w_s = w_all
    wcol = lambda col: W(w_a, 0, col // tn, lead)
    g_hg, g_ssm, g_gla = mix_norm_g[:HG_WIDTH], mix_norm_g[HG_WIDTH:HG_WIDTH + SSM_WIDTH], \
        mix_norm_g[HG_WIDTH + SSM_WIDTH:]

    hg_q = fused_matmul([hb], [wcol(COL_HG_Q)], functools.partial(_ep_scaled_silu, HG_DK ** -0.5), [F32],
                        HG_WIDTH, name="proj_hgq")[0]
    hg_logf, hg_k = fused_matmul([hb], [wcol(COL_HG_F)], _ep_hgrn_forget, [F32, F32], HG_WIDTH,
                                 rows=[lb[None, :]], name="proj_hgf")
    id1 = fused_matmul([hb], [wcol(COL_ID1)], _ep_identity, [F32], ID1_WIDTH, name="proj_id1")[0]
    xbc = conv_matmul(hb, wcol(COL_XBC), conv_w, conv_b[None, :], SSM_XBC)
    small = fused_matmul([hb], [W(w_s, 0, 0, lead)], _ep_identity, [F32], LANES, tn=LANES, name="proj_small")[0]
    id2 = fused_matmul([hb], [W(w_b, 0, 0, lead)], _ep_identity, [F32], ID2_WIDTH, name="proj_id2")[0]
    log_alpha = fused_matmul([small], [W(gate_w, 0, 0, lead)], _ep_log_alpha, [F32], GLA_HEADS * GLA_DK,
                             rows=[gla_b_gate[None, :]], name="gla_gate")[0]

    hb_hg = 6
    y_hg = gla_recurrence(hg_q, hg_k, id1, hg_logf, id1, g_hg[None, :], heads=HG_HEADS, hb=hb_hg, dk=HG_DK,
                          dv=HG_DK, gate_act=_sigmoid, offs=(0, 0, 0, 0, HG_WIDTH // (hb_hg * HG_DK)),
                          name="hgrn")
    y_ssm = ssd_recurrence(xbc, id1, 2 * HG_WIDTH // (SSD_PAIR * SSM_HPG * SSM_HEADDIM), small[:, :SSM_HEADS],
                           dt_bias, a_log, d_skip, g_ssm)
    hb_gla = 4
    wk, wv = hb_gla * GLA_DK, hb_gla * GLA_DV
    y_gla = gla_recurrence(id2, id2, id2, log_alpha, id2, g_gla[None, :], heads=GLA_HEADS, hb=hb_gla, dk=GLA_DK,
                           dv=GLA_DV, gate_act=_silu, q_scale=GLA_DK ** -0.5,
                           offs=(0, GLA_HEADS * GLA_DK // wk, 2 * GLA_HEADS * GLA_DK // wv, 0,
                                 (2 * GLA_HEADS * GLA_DK + GLA_WIDTH) // wv), name="gla")

    tn_m = 256
    nb = D_MODEL // tn_m
    merged = fused_matmul(
        [hb, y_hg, y_ssm, y_gla],
        [W(w_c, 0, 0, lead), W(w_c, 0, nb, lead), W(w_c, 0, 2 * nb, lead),
         W(w_branch, 1, 0, lead, HG_WIDTH, 0), W(w_branch, 2, 0, lead, SSM_WIDTH, 1),
         W(w_branch, 3, 0, lead, GLA_WIDTH, (HG_WIDTH + SSM_WIDTH) // GLA_WIDTH)],
        _ep_merge, [BF16], D_MODEL, tm=512, tn=tn_m, name="merge")[0]
    return merged


def kernel(x, p, ffn_w_in, ffn_w_out, ln_g, ln_b, w_in, hg_lb_logits, ssm_conv_w, ssm_conv_b, ssm_dt_bias,
           ssm_a_log, ssm_d, gla_w_gate, gla_b_gate, mix_norm_g, w_branch, w_out, ple_w_proj, ple_w_gate):
    bsz, t, d = x.shape
    lb_all = lower_bounds(hg_lb_logits.astype(F32))
    ffn_w_in_b = ffn_w_in.astype(BF16)
    ffn_w_out_b = ffn_w_out.astype(BF16)
    w_all = _regroup_w_in(w_in)
    w_branch_b = w_branch.astype(BF16)
    w_out_b = w_out.astype(BF16)
    ple_w_gate_b = ple_w_gate.astype(BF16)
    ple_w_proj_b = ple_w_proj.astype(BF16)
    gate_w = jnp.zeros((DEPTH, LANES, GLA_HEADS * GLA_DK), F32).at[:, SSM_HEADS:SSM_HEADS + GLA_RANK].set(
        gla_w_gate).astype(BF16)
    outs = []
    for bi in range(bsz):
        h = x[bi]
        hb = h.astype(BF16)
        for i in range(DEPTH):
            h, hb = _ffn(h, hb, ffn_w_in_b, ffn_w_out_b, (i, 0), ln_g[i, 0], ln_b[i, 0])
            merged = _mixer(hb, i, lb_all[i], w_all, ssm_conv_w[i], ssm_conv_b[i], ssm_dt_bias[i], ssm_a_log[i],
                            ssm_d[i], gate_w, gla_b_gate[i], mix_norm_g[i], w_branch_b)
            h, hb = ln_matmul(merged, W(w_out_b, 0, 0, (i,)), h, ln_g[i, 1][None, :], ln_b[i, 1][None, :], 1.0)
            h, hb = _ffn(h, hb, ffn_w_in_b, ffn_w_out_b, (i, 1), ln_g[i, 2], ln_b[i, 2])
            h, hb = fused_matmul([hb, p[i, bi]], [W(ple_w_gate_b, 0, 0, (i,)), W(ple_w_proj_b, 1, 0, (i,))],
                                 _ep_ple, [F32, BF16], D_MODEL, tiles=[h], name="ple")
        outs.append(h)
    return jnp.stack(outs, axis=0)
```

```python
import collections
import functools

import jax
import jax.numpy as jnp
import numpy as np
from jax import lax
from jax.experimental import pallas as pl
from jax.experimental.pallas import tpu as pltpu

F32 = jnp.float32
BF16 = jnp.bfloat16

D_MODEL = 4096
DEPTH = 4
P_DIM = 256
D_FF = 5 * D_MODEL // 8
HG_WIDTH = 1536
HG_DK = 128
HG_HEADS = 12
SSM_WIDTH = 1536
SSM_HEADDIM = 64
SSM_HEADS = 24
SSM_GROUPS = 8
SSM_HPG = 3
SSM_STATE = 128
SSM_CONV = 4
SSM_XBC = SSM_WIDTH + 2 * SSM_GROUPS * SSM_STATE
GLA_WIDTH = 1024
GLA_HEADS = 4
GLA_DV = 256
GLA_DK = 128
GLA_RANK = 16
GLA_TAU = 16.0
IN_SPLITS = (HG_WIDTH, HG_WIDTH, HG_WIDTH, HG_WIDTH, SSM_WIDTH, SSM_XBC, SSM_HEADS,
             GLA_HEADS * GLA_DK, GLA_HEADS * GLA_DK, GLA_WIDTH, GLA_WIDTH, GLA_RANK, 3 * D_MODEL)
IN_OFFSETS = tuple(int(o) for o in np.cumsum((0,) + IN_SPLITS))
ALPHA = float((2 * DEPTH) ** 0.25)
LN_EPS = 1e-5
RMS_EPS = 1e-6
LOG2E = float(np.log2(np.e))

VMEM_LIMIT_BYTES = 56 * 1024 * 1024
LANES = 128
SUBLANES = 8

GLA_CHUNK = 64
SSD_CHUNK = 128
SSD_PAIR = 2

COL_HG_Q = 0
COL_HG_F = HG_WIDTH
COL_ID1 = 2 * HG_WIDTH
COL_XBC = COL_ID1 + 2 * HG_WIDTH + SSM_WIDTH
ID1_WIDTH = 2 * HG_WIDTH + SSM_WIDTH
ID2_WIDTH = 2 * GLA_HEADS * GLA_DK + 2 * GLA_WIDTH

W = collections.namedtuple("W", "w x col lead krows kblk", defaults=(0, (), None, 0))


def _cparams(*sem):
    return pltpu.CompilerParams(dimension_semantics=sem, vmem_limit_bytes=VMEM_LIMIT_BYTES)


def _sigmoid(x):
    return 1.0 / (1.0 + jnp.exp(-x))


def _silu(x):
    return x * _sigmoid(x)


def _log_sigmoid(x):
    return jnp.minimum(x, 0.0) - jnp.log1p(jnp.exp(-jnp.abs(x)))


def _softplus(x):
    return jnp.maximum(x, 0.0) + jnp.log1p(jnp.exp(-jnp.abs(x)))


def _dot(a, b):
    return jnp.dot(a, b, preferred_element_type=F32)


def _dot_nt(a, b):
    return lax.dot_general(a, b, (((1,), (1,)), ((), ())), preferred_element_type=F32)


def _dot_tn(a, b):
    return lax.dot_general(a, b, (((0,), (0,)), ((), ())), preferred_element_type=F32)


def _split3(x):
    x1 = x.astype(BF16)
    r1 = x - x1.astype(F32)
    x2 = r1.astype(BF16)
    x3 = (r1 - x2.astype(F32)).astype(BF16)
    return x1, x2, x3


def _prefix_rows(tril_bf16, x):
    x1, x2, x3 = _split3(x)
    return (_dot(tril_bf16, x1) + _dot(tril_bf16, x2)) + _dot(tril_bf16, x3)


def _prefix_lanes(x, triu_bf16):
    x1, x2, x3 = _split3(x)
    return (_dot(x1, triu_bf16) + _dot(x2, triu_bf16)) + _dot(x3, triu_bf16)


def _wspec(w, tn, nargs=2):
    lead = tuple(w.lead)
    krows = w.w.shape[-2] if w.krows is None else w.krows
    return pl.BlockSpec((None,) * len(lead) + (krows, tn),
                        lambda i, j: lead + (w.kblk, j + w.col))


def _mm_kernel(*refs, n_x, w_x, n_rows, n_tiles, n_out, epilogue, tm, sub_m):
    x_refs = refs[:n_x]
    w_refs = refs[n_x:n_x + len(w_x)]
    pos = n_x + len(w_x)
    row_refs = refs[pos:pos + n_rows]
    pos += n_rows
    tile_refs = refs[pos:pos + n_tiles]
    pos += n_tiles
    out_refs = refs[pos:pos + n_out]

    wvals = [None if w.dtype == BF16 else w[...].astype(BF16) for w in w_refs]

    def sub(r, carry):
        rs = pl.ds(pl.multiple_of(r * sub_m, sub_m), sub_m)
        xs = [x[rs, :].astype(BF16) for x in x_refs]
        accs = [_dot(xs[xi], w[...] if wv is None else wv) for w, wv, xi in zip(w_refs, wvals, w_x)]
        outs = epilogue(accs, [rr[...] for rr in row_refs], [t[rs, :] for t in tile_refs])
        for o_ref, o in zip(out_refs, outs):
            o_ref[rs, :] = o.astype(o_ref.dtype)
        return carry

    lax.fori_loop(0, tm // sub_m, sub, 0, unroll=True)


def fused_matmul(xs, ws, epilogue, out_dtypes, n_cols, *, rows=(), tiles=(), tm=1024, tn=512,
                 sub_m=256, name="mm"):
    m = xs[0].shape[0]
    tm = min(tm, m)
    sub_m = min(sub_m, tm)
    grid = (m // tm, n_cols // tn)
    in_specs = [pl.BlockSpec((tm, x.shape[1]), lambda i, j: (i, 0)) for x in xs]
    in_specs += [_wspec(w, tn) for w in ws]
    in_specs += [pl.BlockSpec((1, tn), lambda i, j: (0, j)) for _ in rows]
    in_specs += [pl.BlockSpec((tm, tn), lambda i, j: (i, j)) for _ in tiles]
    out_specs = [pl.BlockSpec((tm, tn), lambda i, j: (i, j)) for _ in out_dtypes]
    out_shape = [jax.ShapeDtypeStruct((m, n_cols), dt) for dt in out_dtypes]
    body = functools.partial(_mm_kernel, n_x=len(xs), w_x=tuple(w.x for w in ws), n_rows=len(rows),
                             n_tiles=len(tiles), n_out=len(out_dtypes), epilogue=epilogue, tm=tm, sub_m=sub_m)
    return pl.pallas_call(
        body, grid=grid, in_specs=in_specs, out_specs=out_specs, out_shape=out_shape,
        compiler_params=_cparams("parallel", "arbitrary"), name=name,
    )(*xs, *[w.w for w in ws], *rows, *tiles)


def _conv_mm_kernel(x_ref, w_ref, cw_ref, cb_ref, o_ref, carry_ref, *, tm, tn, sub_m):
    i = pl.program_id(0)
    j = pl.program_id(1)
    cols = pl.ds(pl.multiple_of(j * tn, tn), tn)

    @pl.when(i == 0)
    def _():
        carry_ref[:, cols] = jnp.zeros((SUBLANES, tn), F32)

    prev = carry_ref[:, cols]
    for r in range(tm // sub_m):
        lo = r * sub_m
        z = _dot(x_ref[lo:lo + sub_m, :], w_ref[...])
        zext = jnp.concatenate([prev, z], axis=0)
        acc = z * cw_ref[SSM_CONV - 1:SSM_CONV, :] + cb_ref[...]
        for back in range(1, SSM_CONV):
            k = SSM_CONV - 1 - back
            acc = acc + pltpu.roll(zext, back, 0)[SUBLANES:, :] * cw_ref[k:k + 1, :]
        o_ref[lo:lo + sub_m, :] = _silu(acc)
        prev = z[sub_m - SUBLANES:, :]
    carry_ref[:, cols] = prev


def conv_matmul(x, w, conv_w, conv_b, n, *, tm=1024, tn=512, sub_m=256):
    m, k = x.shape
    tm = min(tm, m)
    sub_m = min(sub_m, tm)
    body = functools.partial(_conv_mm_kernel, tm=tm, tn=tn, sub_m=sub_m)
    return pl.pallas_call(
        body, grid=(m // tm, n // tn),
        in_specs=[pl.BlockSpec((tm, k), lambda i, j: (i, 0)),
                  _wspec(w, tn),
                  pl.BlockSpec((SSM_CONV, tn), lambda i, j: (0, j)),
                  pl.BlockSpec((1, tn), lambda i, j: (0, j))],
        out_specs=pl.BlockSpec((tm, tn), lambda i, j: (i, j)),
        out_shape=jax.ShapeDtypeStruct((m, n), F32),
        scratch_shapes=[pltpu.VMEM((SUBLANES, n), F32)],
        compiler_params=_cparams("arbitrary", "arbitrary"), name="conv_mm",
    )(x, w.w, conv_w, conv_b)


def _ln_kernel(z_ref, g_ref, b_ref, o_ref, obf_ref, *, tr, ln_m):
    def norm(r, carry):
        rs = pl.ds(pl.multiple_of(r * ln_m, ln_m), ln_m)
        z = z_ref[rs, :]
        mu = jnp.mean(z, axis=-1, keepdims=True)
        zc = z - mu
        var = jnp.mean(zc * zc, axis=-1, keepdims=True)
        y = zc * lax.rsqrt(var + LN_EPS) * g_ref[...] + b_ref[...]
        o_ref[rs, :] = y
        obf_ref[rs, :] = y.astype(BF16)
        return carry

    lax.fori_loop(0, tr // ln_m, norm, 0, unroll=4)


def layer_norm_rows(z, gamma, beta, *, tr=256, ln_m=8):
    m, n = z.shape
    tr = min(tr, m)
    row = pl.BlockSpec((tr, n), lambda i: (i, 0))
    vec = pl.BlockSpec((1, n), lambda i: (0, 0))
    return pl.pallas_call(
        functools.partial(_ln_kernel, tr=tr, ln_m=ln_m), grid=(m // tr,),
        in_specs=[row, vec, vec], out_specs=[row, row],
        out_shape=[jax.ShapeDtypeStruct((m, n), F32), jax.ShapeDtypeStruct((m, n), BF16)],
        compiler_params=_cparams("parallel"), name="ln_rows",
    )(z, gamma, beta)


def _ep_resid(scale, accs, rows, tiles):
    return (ALPHA * tiles[0] + scale * accs[0],)


def ln_matmul(x, w, resid, gamma, beta, scale):
    z = fused_matmul([x], [w], functools.partial(_ep_resid, scale), [F32], resid.shape[1], tiles=[resid],
                     name="resid_mm")[0]
    return layer_norm_rows(z, gamma, beta)


def _lb_kernel(x_ref, lb_ref):
    x = x_ref[...]
    mx = jnp.max(x, axis=0, keepdims=True)
    e = jnp.exp(x - mx)
    p = e / jnp.sum(e, axis=0, keepdims=True)
    c = p[0:1, :]
    first = c
    lb_ref[0:1, :] = c - first
    for i in range(1, DEPTH):
        c = c + p[i:i + 1, :]
        lb_ref[i:i + 1, :] = c - first


def lower_bounds(logits):
    return pl.pallas_call(_lb_kernel, out_shape=jax.ShapeDtypeStruct(logits.shape, F32), name="hgrn_lb")(logits)


def _gla_levels(chunk):
    hs = []
    h = SUBLANES
    while h < chunk:
        hs.append(h)
        h *= 2
    return hs


def _gla_masks(chunk, dk):
    c = chunk
    t = np.arange(c)
    tril = (t[:, None] >= t[None, :]).astype(np.float32)
    pair, sgn = [], []
    for h in _gla_levels(c):
        is_q = (t % (2 * h)) >= h
        same = (t[:, None] // (2 * h)) == (t[None, :] // (2 * h))
        pair.append((same & is_q[:, None] & ~is_q[None, :]).astype(np.float32))
        sgn.append(np.broadcast_to(np.where(is_q, 1.0, -1.0)[:, None], (c, dk)).astype(np.float32))
    lane = np.arange(LANES)
    esel = np.tile(((lane[None, :] % SUBLANES) == np.arange(SUBLANES)[:, None]).astype(np.float32)[:, None, :],
                   (1, dk, 1)).reshape(SUBLANES * dk, LANES)
    diag = ((lane[None, :] < c) & (lane[None, :] // SUBLANES == t[:, None] // SUBLANES)
            & (t[:, None] % SUBLANES >= lane[None, :] % SUBLANES)).astype(np.float32)
    return (jnp.asarray(tril, BF16), jnp.asarray(np.stack(pair)), jnp.asarray(np.stack(sgn)),
            jnp.asarray(esel, BF16), jnp.asarray(diag))


def _gla_kernel(q_ref, k_ref, v_ref, g_ref, gate_ref, gamma_ref, tril_ref, pair_ref, sgn_ref, esel_ref, diag_ref,
                o_ref, st_ref, bs_ref, sc_ref, *, chunk, n_chunks, hb, dk, dv, q_scale, gate_act):
    c = chunk
    nblk = c // SUBLANES
    levels = _gla_levels(c)

    @pl.when(pl.program_id(1) == 0)
    def _():
        st_ref[...] = jnp.zeros(st_ref.shape, F32)

    heads = range(hb)
    kcs = [slice(hh * dk, (hh + 1) * dk) for hh in heads]
    vcs = [slice(hh * dv, (hh + 1) * dv) for hh in heads]

    def rows_of(ci):
        return pl.ds(ci * c if isinstance(ci, int) else pl.multiple_of(ci * c, c), c)

    def load_q(rs, hh):
        q = q_ref[rs, kcs[hh]]
        return q * q_scale if q_scale != 1.0 else q

    def front_prefix(ci, s):
        rs = rows_of(ci)
        tril = tril_ref[...]
        for hh in heads:
            bs_ref[s, hh, 0] = _prefix_rows(tril, g_ref[rs, kcs[hh]]) * LOG2E
            bs_ref[s, hh, 1] = k_ref[rs, kcs[hh]]

    def front_diag(ci, s):
        rs = rows_of(ci)
        scores = []
        for hh in heads:
            q = load_q(rs, hh)
            b2 = bs_ref[s, hh, 0]
            xs = []
            for j in range(SUBLANES):
                blocks = []
                for p in range(nblk):
                    r = p * SUBLANES + j
                    rows = slice(p * SUBLANES, (p + 1) * SUBLANES)
                    bj = bs_ref[s, hh, 0, r:r + 1, :]
                    kj = bs_ref[s, hh, 1, r:r + 1, :]
                    blocks.append(q[rows] * kj * jnp.exp2(jnp.minimum(b2[rows] - bj, 0.0)))
                xs.append(jnp.concatenate(blocks, axis=0).astype(BF16))
            scores.append((_dot(jnp.concatenate(xs, axis=1), esel_ref[...]) * diag_ref[...])[:, :c])
        return scores

    def front_tree(ci, s, scores):
        rs = rows_of(ci)
        for li, h in enumerate(levels):
            for hh in heads:
                q = load_q(rs, hh)
                k = bs_ref[s, hh, 1]
                b2 = bs_ref[s, hh, 0]
                ref = jnp.concatenate(
                    [jnp.broadcast_to(bs_ref[s, hh, 0, m * 2 * h + h - 1:m * 2 * h + h, :], (2 * h, dk))
                     for m in range(c // (2 * h))], axis=0)
                sg = sgn_ref[li]
                z = (jnp.where(sg > 0.0, q, k) * jnp.exp2(sg * (b2 - ref))).astype(BF16)
                scores[hh] = scores[hh] + _dot_nt(z, z) * pair_ref[li]
        for hh in heads:
            sc_ref[s, hh] = scores[hh].astype(BF16)

    def back_state(ci, s):
        rs = rows_of(ci)
        return [_dot_nt((load_q(rs, hh) * jnp.exp2(bs_ref[s, hh, 0])).astype(BF16), st_ref[hh].astype(BF16))
                for hh in heads]

    def back_out(ci, s, o_inter):
        rs = rows_of(ci)
        return [o_inter[hh] + _dot(sc_ref[s, hh], v_ref[rs, vcs[hh]].astype(BF16)) for hh in heads]

    def back_update(ci, s, outs):
        rs = rows_of(ci)
        for hh in heads:
            vb = v_ref[rs, vcs[hh]].astype(BF16)
            b_last = bs_ref[s, hh, 0, c - 1:c, :]
            kd = (bs_ref[s, hh, 1] * jnp.exp2(b_last - bs_ref[s, hh, 0])).astype(BF16)
            st_ref[hh] = st_ref[hh] * jnp.exp2(b_last) + _dot_tn(vb, kd)
        for hh in heads:
            o = outs[hh]
            ms = jnp.mean(o * o, axis=-1, keepdims=True)
            y = o * lax.rsqrt(ms + RMS_EPS) * gamma_ref[:, vcs[hh]] * gate_act(gate_ref[rs, vcs[hh]])
            o_ref[rs, vcs[hh]] = y.astype(o_ref.dtype)

    def overlapped(cb, sb, cf, sf):
        o_inter = back_state(cb, sb)
        front_prefix(cf, sf)
        outs = back_out(cb, sb, o_inter)
        scores = front_diag(cf, sf)
        back_update(cb, sb, outs)
        front_tree(cf, sf, scores)

    front_prefix(0, 0)
    front_tree(0, 0, front_diag(0, 0))

    def step(i, carry):
        overlapped(2 * i, 0, 2 * i + 1, 1)
        overlapped(2 * i + 1, 1, jnp.minimum(2 * i + 2, n_chunks - 1), 0)
        return carry

    lax.fori_loop(0, n_chunks // 2, step, 0)


def gla_recurrence(q, k, v, g, gate, gamma, *, heads, hb, dk, dv, gate_act, q_scale=1.0, offs=(0, 0, 0, 0, 0),
                   tb=1024, chunk=GLA_CHUNK, name="gla"):
    t = q.shape[0]
    tb = min(tb, t)
    masks = _gla_masks(chunk, dk)
    body = functools.partial(_gla_kernel, chunk=chunk, n_chunks=tb // chunk, hb=hb, dk=dk, dv=dv,
                             q_scale=q_scale, gate_act=gate_act)
    kspec = lambda off: pl.BlockSpec((tb, hb * dk), lambda h, i: (i, h + off))
    vspec = lambda off: pl.BlockSpec((tb, hb * dv), lambda h, i: (i, h + off))
    full = lambda a: pl.BlockSpec(a.shape, lambda h, i: (0,) * a.ndim)
    return pl.pallas_call(
        body, grid=(heads // hb, t // tb),
        in_specs=[kspec(offs[0]), kspec(offs[1]), vspec(offs[2]), kspec(offs[3]), vspec(offs[4]),
                  pl.BlockSpec((1, hb * dv), lambda h, i: (0, h))] + [full(a) for a in masks],
        out_specs=vspec(0),
        out_shape=jax.ShapeDtypeStruct((t, heads * dv), BF16),
        scratch_shapes=[pltpu.VMEM((hb, dv, dk), F32), pltpu.VMEM((2, hb, 2, chunk, dk), F32),
                        pltpu.VMEM((2, hb, chunk, chunk), BF16)],
        compiler_params=_cparams("parallel", "arbitrary"), name=name,
    )(q, k, v, g, gate, gamma, *masks)


SSD_HP = SSD_PAIR * SSM_HPG
SSD_ROWS = SUBLANES


def _ssd_bcast_matrix():
    sel = np.zeros((6 * SSD_ROWS, 2 * SSD_HP * LANES), np.float32)
    for h in range(SSD_HP):
        for part in range(3):
            sel[part * SSD_ROWS + h, h * LANES:(h + 1) * LANES] = 1.0
        for part in range(2):
            sel[(3 + part) * SSD_ROWS + h, (SSD_HP + h) * LANES:(SSD_HP + h + 1) * LANES] = 1.0
    return jnp.asarray(sel, BF16)


def _ssd_kernel(x_ref, b_ref, c_ref, z_ref, dtt_ref, biast_ref, at_ref, d_ref, gamma_ref, sel_ref,
                o_ref, st_ref, rowq_ref, colb_ref, *, n_chunks):
    c = SSD_CHUNK
    p = SSM_HEADDIM
    n = SSM_STATE
    npr = SSD_HP // 2

    @pl.when(pl.program_id(1) == 0)
    def _():
        st_ref[...] = jnp.zeros(st_ref.shape, F32)

    r2 = lax.broadcasted_iota(jnp.int32, (c, c), 0)
    c2 = lax.broadcasted_iota(jnp.int32, (c, c), 1)
    causal = r2 >= c2
    low = c2 < p
    top = r2 < p
    triu = (r2 <= c2).astype(BF16)
    biast = biast_ref[...]
    a_col = at_ref[...] * LOG2E

    tb = n_chunks * c
    dtt_st = jnp.concatenate([_softplus(dtt_ref[ci] + biast) for ci in range(n_chunks)], axis=0)
    e1, e2, e3 = _split3(dtt_st * jnp.concatenate([a_col] * n_chunks, axis=0))
    pre = _dot(jnp.concatenate([e1, e2, e3], axis=0), triu)
    nr = n_chunks * SSD_ROWS
    cst_st = (pre[:nr] + pre[nr:2 * nr]) + pre[2 * nr:]
    unstack = lambda v_: jnp.concatenate([v_[ci * SSD_ROWS:(ci + 1) * SSD_ROWS] for ci in range(n_chunks)], axis=1)
    dtt_all = unstack(dtt_st)
    cst_all = unstack(cst_st)
    c1, c2_, c3 = _split3(cst_all)
    d1 = dtt_all.astype(BF16)
    d2 = (dtt_all - d1.astype(F32)).astype(BF16)
    parts = jnp.concatenate([v_.astype(F32) for v_ in (c1, c2_, c3, d1, d2)]
                            + [jnp.zeros((SSD_ROWS, tb), F32)], axis=0).astype(BF16)
    colb_all = _dot_tn(parts, sel_ref[...])
    for ci in range(n_chunks):
        rowq_ref[ci, 0] = dtt_all[:, ci * c:(ci + 1) * c]
        rowq_ref[ci, 1] = cst_all[:, ci * c:(ci + 1) * c]
        colb_ref[ci] = colb_all[ci * c:(ci + 1) * c]

    def step(ci, carry):
        rs = pl.ds(pl.multiple_of(ci * c, c), c)
        dtt = rowq_ref[ci, 0]
        cst = rowq_ref[ci, 1]
        colb = colb_ref.at[ci]
        bis = [b_ref[rs, gl * n:(gl + 1) * n].astype(BF16) for gl in range(SSD_PAIR)]
        cms = [c_ref[rs, gl * n:(gl + 1) * n].astype(BF16) for gl in range(SSD_PAIR)]
        cbs = [_dot_nt(cms[gl], bis[gl]) for gl in range(SSD_PAIR)]

        ypairs = []
        for pr in range(npr):
            lanes = slice(pr * LANES, (pr + 1) * LANES)
            h0, h1 = 2 * pr, 2 * pr + 1
            g0, g1 = h0 // SSM_HPG, h1 // SSM_HPG
            xp = x_ref[rs, lanes]
            xpb = xp.astype(BF16)
            ysel = []
            for h in (h0, h1):
                cs_b = colb[:, h * LANES:(h + 1) * LANES]
                lmat = jnp.where(causal, jnp.exp2(jnp.minimum(cs_b - cst[h:h + 1, :], 0.0)), 0.0)
                m = (cbs[h // SSM_HPG] * lmat * dtt[h:h + 1, :]).astype(BF16)
                ysel.append(_dot(m, xpb))
            y = jnp.where(low, ysel[0], ysel[1])
            csb = jnp.where(low, colb[:, h0 * LANES:(h0 + 1) * LANES], colb[:, h1 * LANES:(h1 + 1) * LANES])
            dtb = jnp.where(low, colb[:, (SSD_HP + h0) * LANES:(SSD_HP + h0 + 1) * LANES],
                            colb[:, (SSD_HP + h1) * LANES:(SSD_HP + h1 + 1) * LANES])
            st = st_ref[pr]
            stb = st.astype(BF16)
            ystate = _dot_nt(cms[g0], stb)
            if g1 != g0:
                ystate = jnp.where(low, ystate, _dot_nt(cms[g1], stb))
            y = y + ystate * jnp.exp2(csb)
            last = csb[c - 1:c, :]
            xu = (xp * (dtb * jnp.exp2(last - csb))).astype(BF16)
            upd = _dot_tn(xu, bis[g0])
            if g1 != g0:
                upd = jnp.where(top, upd, _dot_tn(xu, bis[g1]))
            dec = jnp.where(top, jnp.exp2(cst[h0:h0 + 1, c - 1:c]), jnp.exp2(cst[h1:h1 + 1, c - 1:c]))
            st_ref[pr] = st * dec + upd
            ypairs.append((y + d_ref[:, lanes] * xp) * _silu(z_ref[rs, lanes]))

        sq = [y * y for y in ypairs]
        zero = jnp.zeros((c, c), F32)
        ss0 = jnp.sum(sq[0] + jnp.where(low, sq[1], zero), axis=-1, keepdims=True)
        ss1 = jnp.sum(sq[2] + jnp.where(low, zero, sq[1]), axis=-1, keepdims=True)
        scale = 1.0 / (SSM_HPG * p)
        r0 = lax.rsqrt(ss0 * scale + RMS_EPS)
        r1 = lax.rsqrt(ss1 * scale + RMS_EPS)
        rinv = [r0, jnp.where(low, r0, r1), r1]
        for pr in range(npr):
            lanes = slice(pr * LANES, (pr + 1) * LANES)
            o_ref[rs, lanes] = (ypairs[pr] * rinv[pr] * gamma_ref[:, lanes]).astype(o_ref.dtype)
        return carry

    lax.fori_loop(0, n_chunks, step, 0)


def ssd_recurrence(xbc, z, z_off, dt_raw, dt_bias, a_log, d_skip, gamma, *, tb=512):
    assert SSD_PAIR == 2 and SSM_HPG == 3 and 2 * SSM_HEADDIM == LANES and SSD_CHUNK == LANES
    t = xbc.shape[0]
    tb = min(tb, t)
    c = SSD_CHUNK
    npair = SSM_GROUPS // SSD_PAIR
    hp = SSD_HP
    wx = hp * SSM_HEADDIM
    wn = SSD_PAIR * SSM_STATE
    padr = lambda v_: jnp.pad(v_, [(0, 0)] * (v_.ndim - 2) + [(0, SSD_ROWS - hp), (0, 0)])
    dtt_p = padr(dt_raw.reshape(t // c, c, npair, hp).transpose(2, 0, 3, 1))
    a = -jnp.exp(a_log.astype(F32))
    colp = lambda v_: padr(v_.reshape(npair, hp, 1))
    d_lanes = jnp.repeat(d_skip.astype(F32), SSM_HEADDIM).reshape(1, SSM_WIDTH)
    sel = _ssd_bcast_matrix()
    body = functools.partial(_ssd_kernel, n_chunks=tb // c)
    b_off = SSM_WIDTH // wn
    c_off = (SSM_WIDTH + SSM_GROUPS * SSM_STATE) // wn
    pspec_col = pl.BlockSpec((None, SSD_ROWS, 1), lambda g, i: (g, 0, 0))
    lane_row = pl.BlockSpec((1, wx), lambda g, i: (0, g))
    return pl.pallas_call(
        body, grid=(npair, t // tb),
        in_specs=[pl.BlockSpec((tb, wx), lambda g, i: (i, g)),
                  pl.BlockSpec((tb, wn), lambda g, i: (i, g + b_off)),
                  pl.BlockSpec((tb, wn), lambda g, i: (i, g + c_off)),
                  pl.BlockSpec((tb, wx), lambda g, i: (i, g + z_off)),
                  pl.BlockSpec((None, tb // c, SSD_ROWS, c), lambda g, i: (g, i, 0, 0)),
                  pspec_col, pspec_col, lane_row, lane_row,
                  pl.BlockSpec(sel.shape, lambda g, i: (0, 0))],
        out_specs=pl.BlockSpec((tb, wx), lambda g, i: (i, g)),
        out_shape=jax.ShapeDtypeStruct((t, SSM_WIDTH), BF16),
        scratch_shapes=[pltpu.VMEM((hp // 2, 2 * SSM_HEADDIM, SSM_STATE), F32),
                        pltpu.VMEM((tb // c, 2, SSD_ROWS, c), F32),
                        pltpu.VMEM((tb // c, c, 2 * hp * LANES), F32)],
        compiler_params=_cparams("parallel", "arbitrary"), name="ssd",
    )(xbc, xbc, xbc, z, dtt_p, colp(dt_bias), colp(a), d_lanes, gamma.reshape(1, SSM_WIDTH), sel)


def _ep_swiglu(accs, rows, tiles):
    return (_silu(accs[0]) * accs[1],)


def _ep_identity(accs, rows, tiles):
    return (accs[0],)


def _ep_scaled_silu(scale, accs, rows, tiles):
    return (_silu(accs[0]) * scale,)


def _ep_hgrn_forget(accs, rows, tiles):
    zf = accs[0]
    lb = rows[0]
    la = jnp.log(lb)
    lc = jnp.log1p(-lb) + _log_sigmoid(zf)
    log_f = jnp.maximum(la, lc) + jnp.log1p(jnp.exp(-jnp.abs(la - lc)))
    return (log_f, (1.0 - lb) * _sigmoid(-zf))


def _ep_log_alpha(accs, rows, tiles):
    return (_log_sigmoid(accs[0] + rows[0]) * (1.0 / GLA_TAU),)


def _ep_merge(accs, rows, tiles):
    return (_sigmoid(accs[0]) * accs[3] + _sigmoid(accs[1]) * accs[4] + _sigmoid(accs[2]) * accs[5],)


def _ep_ple(accs, rows, tiles):
    y = tiles[0] + _sigmoid(accs[0]) * accs[1]
    return (y, y)


def _regroup_w_in(w_in):
    o = IN_OFFSETS
    w = w_in.astype(BF16)
    small = jnp.concatenate([w[..., o[6]:o[7]], w[..., o[11]:o[12]]], axis=-1)
    small = jnp.pad(small, ((0, 0), (0, 0), (0, LANES - small.shape[-1])))
    return w, w[..., o[7]:o[11]], w[..., o[12]:], small


def _ffn(h, hb, w_in, w_out, lead, g, b):
    mid = fused_matmul([hb], [W(w_in, 0, 0, lead), W(w_in, 0, D_FF // 512, lead)], _ep_swiglu, [BF16], D_FF,
                       name="ffn_in")[0]
    return ln_matmul(mid, W(w_out, 0, 0, lead), h, g[None, :], b[None, :], 0.5)


def _mixer(hb, layer, lb, w_all, conv_w, conv_b, dt_bias, a_log, d_skip, gate_w, gla_b_gate, mix_norm_g, w_branch):
    lead = (layer,)
    tn = 512
    w_a, w_b, w_c, w_s = w_all
    wcol = lambda col: W(w_a, 0, col // tn, lead)
    g_hg, g_ssm, g_gla = mix_norm_g[:HG_WIDTH], mix_norm_g[HG_WIDTH:HG_WIDTH + SSM_WIDTH], \
        mix_norm_g[HG_WIDTH + SSM_WIDTH:]

    hg_q = fused_matmul([hb], [wcol(COL_HG_Q)], functools.partial(_ep_scaled_silu, HG_DK ** -0.5), [F32],
                        HG_WIDTH, name="proj_hgq")[0]
    hg_logf, hg_k = fused_matmul([hb], [wcol(COL_HG_F)], _ep_hgrn_forget, [F32, F32], HG_WIDTH,
                                 rows=[lb[None, :]], name="proj_hgf")
    id1 = fused_matmul([hb], [wcol(COL_ID1)], _ep_identity, [F32], ID1_WIDTH, name="proj_id1")[0]
    xbc = conv_matmul(hb, wcol(COL_XBC), conv_w, conv_b[None, :], SSM_XBC)
    small = fused_matmul([hb], [W(w_s, 0, 0, lead)], _ep_identity, [F32], LANES, tn=LANES, name="proj_small")[0]
    id2 = fused_matmul([hb], [W(w_b, 0, 0, lead)], _ep_identity, [F32], ID2_WIDTH, name="proj_id2")[0]
    log_alpha = fused_matmul([small], [W(gate_w, 0, 0, lead)], _ep_log_alpha, [F32], GLA_HEADS * GLA_DK,
                             rows=[gla_b_gate[None, :]], name="gla_gate")[0]

    hb_hg = 6
    y_hg = gla_recurrence(hg_q, hg_k, id1, hg_logf, id1, g_hg[None, :], heads=HG_HEADS, hb=hb_hg, dk=HG_DK,
                          dv=HG_DK, gate_act=_sigmoid, offs=(0, 0, 0, 0, HG_WIDTH // (hb_hg * HG_DK)),
                          name="hgrn")
    y_ssm = ssd_recurrence(xbc, id1, 2 * HG_WIDTH // (SSD_PAIR * SSM_HPG * SSM_HEADDIM), small[:, :SSM_HEADS],
                           dt_bias, a_log, d_skip, g_ssm)
    hb_gla = 4
    wk, wv = hb_gla * GLA_DK, hb_gla * GLA_DV
    y_gla = gla_recurrence(id2, id2, id2, log_alpha, id2, g_gla[None, :], heads=GLA_HEADS, hb=hb_gla, dk=GLA_DK,
                           dv=GLA_DV, gate_act=_silu, q_scale=GLA_DK ** -0.5,
                           offs=(0, GLA_HEADS * GLA_DK // wk, 2 * GLA_HEADS * GLA_DK // wv, 0,
                                 (2 * GLA_HEADS * GLA_DK + GLA_WIDTH) // wv), name="gla")

    tn_m = 256
    nb = D_MODEL // tn_m
    merged = fused_matmul(
        [hb, y_hg, y_ssm, y_gla],
        [W(w_c, 0, 0, lead), W(w_c, 0, nb, lead), W(w_c, 0, 2 * nb, lead),
         W(w_branch, 1, 0, lead, HG_WIDTH, 0), W(w_branch, 2, 0, lead, SSM_WIDTH, 1),
         W(w_branch, 3, 0, lead, GLA_WIDTH, (HG_WIDTH + SSM_WIDTH) // GLA_WIDTH)],
        _ep_merge, [BF16], D_MODEL, tm=512, tn=tn_m, name="merge")[0]
    return merged


def kernel(x, p, ffn_w_in, ffn_w_out, ln_g, ln_b, w_in, hg_lb_logits, ssm_conv_w, ssm_conv_b, ssm_dt_bias,
           ssm_a_log, ssm_d, gla_w_gate, gla_b_gate, mix_norm_g, w_branch, w_out, ple_w_proj, ple_w_gate):
    bsz, t, d = x.shape
    lb_all = lower_bounds(hg_lb_logits.astype(F32))
    ffn_w_in_b = ffn_w_in.astype(BF16)
    ffn_w_out_b = ffn_w_out.astype(BF16)
    w_all = _regroup_w_in(w_in)
    w_branch_b = w_branch.astype(BF16)
    w_out_b = w_out.astype(BF16)
    ple_w_gate_b = ple_w_gate.astype(BF16)
    ple_w_proj_b = ple_w_proj.astype(BF16)
    gate_w = jnp.zeros((DEPTH, LANES, GLA_HEADS * GLA_DK), F32).at[:, SSM_HEADS:SSM_HEADS + GLA_RANK].set(
        gla_w_gate).astype(BF16)
    outs = []
    for bi in range(bsz):
        h = x[bi]
        hb = h.astype(BF16)
        for i in range(DEPTH):
            h, hb = _ffn(h, hb, ffn_w_in_b, ffn_w_out_b, (i, 0), ln_g[i, 0], ln_b[i, 0])
            merged = _mixer(hb, i, lb_all[i], w_all, ssm_conv_w[i], ssm_conv_b[i], ssm_dt_bias[i], ssm_a_log[i],
                            ssm_d[i], gate_w, gla_b_gate[i], mix_norm_g[i], w_branch_b)
            h, hb = ln_matmul(merged, W(w_out_b, 0, 0, (i,)), h, ln_g[i, 1][None, :], ln_b[i, 1][None, :], 1.0)
            h, hb = _ffn(h, hb, ffn_w_in_b, ffn_w_out_b, (i, 1), ln_g[i, 2], ln_b[i, 2])
            h, hb = fused_matmul([hb, p[i, bi]], [W(ple_w_gate_b, 0, 0, (i,)), W(ple_w_proj_b, 1, 0, (i,))],
                                 _ep_ple, [F32, BF16], D_MODEL, tiles=[h], name="ple")
        outs.append(h)
    return jnp.stack(outs, axis=0)
```

```python
import collections
import functools

import jax
import jax.numpy as jnp
import numpy as np
from jax import lax
from jax.experimental import pallas as pl
from jax.experimental.pallas import tpu as pltpu

F32 = jnp.float32
BF16 = jnp.bfloat16

D_MODEL = 4096
DEPTH = 4
P_DIM = 256
D_FF = 5 * D_MODEL // 8
HG_WIDTH = 1536
HG_DK = 128
HG_HEADS = 12
SSM_WIDTH = 1536
SSM_HEADDIM = 64
SSM_HEADS = 24
SSM_GROUPS = 8
SSM_HPG = 3
SSM_STATE = 128
SSM_CONV = 4
SSM_XBC = SSM_WIDTH + 2 * SSM_GROUPS * SSM_STATE
GLA_WIDTH = 1024
GLA_HEADS = 4
GLA_DV = 256
GLA_DK = 128
GLA_RANK = 16
GLA_TAU = 16.0
IN_SPLITS = (HG_WIDTH, HG_WIDTH, HG_WIDTH, HG_WIDTH, SSM_WIDTH, SSM_XBC, SSM_HEADS,
             GLA_HEADS * GLA_DK, GLA_HEADS * GLA_DK, GLA_WIDTH, GLA_WIDTH, GLA_RANK, 3 * D_MODEL)
IN_OFFSETS = tuple(int(o) for o in np.cumsum((0,) + IN_SPLITS))
ALPHA = float((2 * DEPTH) ** 0.25)
LN_EPS = 1e-5
RMS_EPS = 1e-6
LOG2E = float(np.log2(np.e))

VMEM_LIMIT_BYTES = 56 * 1024 * 1024
LANES = 128
SUBLANES = 8

GLA_CHUNK = 64
SSD_CHUNK = 128
SSD_PAIR = 2

COL_HG_Q = 0
COL_HG_F = HG_WIDTH
COL_ID1 = 2 * HG_WIDTH
COL_XBC = COL_ID1 + 2 * HG_WIDTH + SSM_WIDTH
ID1_WIDTH = 2 * HG_WIDTH + SSM_WIDTH
ID2_WIDTH = 2 * GLA_HEADS * GLA_DK + 2 * GLA_WIDTH

W = collections.namedtuple("W", "w x col lead krows kblk", defaults=(0, (), None, 0))


def _cparams(*sem):
    return pltpu.CompilerParams(dimension_semantics=sem, vmem_limit_bytes=VMEM_LIMIT_BYTES)


def _sigmoid(x):
    return 1.0 / (1.0 + jnp.exp(-x))


def _silu(x):
    return x * _sigmoid(x)


def _log_sigmoid(x):
    return jnp.minimum(x, 0.0) - jnp.log1p(jnp.exp(-jnp.abs(x)))


def _softplus(x):
    return jnp.maximum(x, 0.0) + jnp.log1p(jnp.exp(-jnp.abs(x)))


def _dot(a, b):
    return jnp.dot(a, b, preferred_element_type=F32)


def _dot_nt(a, b):
    return lax.dot_general(a, b, (((1,), (1,)), ((), ())), preferred_element_type=F32)


def _dot_tn(a, b):
    return lax.dot_general(a, b, (((0,), (0,)), ((), ())), preferred_element_type=F32)


def _split3(x):
    x1 = x.astype(BF16)
    r1 = x - x1.astype(F32)
    x2 = r1.astype(BF16)
    x3 = (r1 - x2.astype(F32)).astype(BF16)
    return x1, x2, x3


def _prefix_rows(tril_bf16, x):
    x1, x2, x3 = _split3(x)
    return (_dot(tril_bf16, x1) + _dot(tril_bf16, x2)) + _dot(tril_bf16, x3)


def _prefix_lanes(x, triu_bf16):
    x1, x2, x3 = _split3(x)
    return (_dot(x1, triu_bf16) + _dot(x2, triu_bf16)) + _dot(x3, triu_bf16)


def _wspec(w, tn, nargs=2):
    lead = tuple(w.lead)
    krows = w.w.shape[-2] if w.krows is None else w.krows
    return pl.BlockSpec((None,) * len(lead) + (krows, tn),
                        lambda i, j: lead + (w.kblk, j + w.col))


def _mm_kernel(*refs, n_x, w_x, n_rows, n_tiles, n_out, epilogue, tm, sub_m):
    x_refs = refs[:n_x]
    w_refs = refs[n_x:n_x + len(w_x)]
    pos = n_x + len(w_x)
    row_refs = refs[pos:pos + n_rows]
    pos += n_rows
    tile_refs = refs[pos:pos + n_tiles]
    pos += n_tiles
    out_refs = refs[pos:pos + n_out]

    wvals = [None if w.dtype == BF16 else w[...].astype(BF16) for w in w_refs]

    def sub(r, carry):
        rs = pl.ds(pl.multiple_of(r * sub_m, sub_m), sub_m)
        xs = [x[rs, :].astype(BF16) for x in x_refs]
        accs = [_dot(xs[xi], w[...] if wv is None else wv) for w, wv, xi in zip(w_refs, wvals, w_x)]
        outs = epilogue(accs, [rr[...] for rr in row_refs], [t[rs, :] for t in tile_refs])
        for o_ref, o in zip(out_refs, outs):
            o_ref[rs, :] = o.astype(o_ref.dtype)
        return carry

    lax.fori_loop(0, tm // sub_m, sub, 0, unroll=True)


def fused_matmul(xs, ws, epilogue, out_dtypes, n_cols, *, rows=(), tiles=(), tm=1024, tn=512,
                 sub_m=256, name="mm"):
    m = xs[0].shape[0]
    tm = min(tm, m)
    sub_m = min(sub_m, tm)
    grid = (m // tm, n_cols // tn)
    in_specs = [pl.BlockSpec((tm, x.shape[1]), lambda i, j: (i, 0)) for x in xs]
    in_specs += [_wspec(w, tn) for w in ws]
    in_specs += [pl.BlockSpec((1, tn), lambda i, j: (0, j)) for _ in rows]
    in_specs += [pl.BlockSpec((tm, tn), lambda i, j: (i, j)) for _ in tiles]
    out_specs = [pl.BlockSpec((tm, tn), lambda i, j: (i, j)) for _ in out_dtypes]
    out_shape = [jax.ShapeDtypeStruct((m, n_cols), dt) for dt in out_dtypes]
    body = functools.partial(_mm_kernel, n_x=len(xs), w_x=tuple(w.x for w in ws), n_rows=len(rows),
                             n_tiles=len(tiles), n_out=len(out_dtypes), epilogue=epilogue, tm=tm, sub_m=sub_m)
    return pl.pallas_call(
        body, grid=grid, in_specs=in_specs, out_specs=out_specs, out_shape=out_shape,
        compiler_params=_cparams("parallel", "arbitrary"), name=name,
    )(*xs, *[w.w for w in ws], *rows, *tiles)


def _conv_mm_kernel(x_ref, w_ref, cw_ref, cb_ref, o_ref, carry_ref, *, tm, tn, sub_m):
    i = pl.program_id(0)
    j = pl.program_id(1)
    cols = pl.ds(pl.multiple_of(j * tn, tn), tn)

    @pl.when(i == 0)
    def _():
        carry_ref[:, cols] = jnp.zeros((SUBLANES, tn), F32)

    prev = carry_ref[:, cols]
    for r in range(tm // sub_m):
        lo = r * sub_m
        z = _dot(x_ref[lo:lo + sub_m, :], w_ref[...])
        zext = jnp.concatenate([prev, z], axis=0)
        acc = z * cw_ref[SSM_CONV - 1:SSM_CONV, :] + cb_ref[...]
        for back in range(1, SSM_CONV):
            k = SSM_CONV - 1 - back
            acc = acc + pltpu.roll(zext, back, 0)[SUBLANES:, :] * cw_ref[k:k + 1, :]
        o_ref[lo:lo + sub_m, :] = _silu(acc)
        prev = z[sub_m - SUBLANES:, :]
    carry_ref[:, cols] = prev


def conv_matmul(x, w, conv_w, conv_b, n, *, tm=1024, tn=512, sub_m=256):
    m, k = x.shape
    tm = min(tm, m)
    sub_m = min(sub_m, tm)
    body = functools.partial(_conv_mm_kernel, tm=tm, tn=tn, sub_m=sub_m)
    return pl.pallas_call(
        body, grid=(m // tm, n // tn),
        in_specs=[pl.BlockSpec((tm, k), lambda i, j: (i, 0)),
                  _wspec(w, tn),
                  pl.BlockSpec((SSM_CONV, tn), lambda i, j: (0, j)),
                  pl.BlockSpec((1, tn), lambda i, j: (0, j))],
        out_specs=pl.BlockSpec((tm, tn), lambda i, j: (i, j)),
        out_shape=jax.ShapeDtypeStruct((m, n), F32),
        scratch_shapes=[pltpu.VMEM((SUBLANES, n), F32)],
        compiler_params=_cparams("arbitrary", "arbitrary"), name="conv_mm",
    )(x, w.w, conv_w, conv_b)


def _ln_kernel(z_ref, g_ref, b_ref, o_ref, obf_ref, *, tr, ln_m):
    def norm(r, carry):
        rs = pl.ds(pl.multiple_of(r * ln_m, ln_m), ln_m)
        z = z_ref[rs, :]
        mu = jnp.mean(z, axis=-1, keepdims=True)
        zc = z - mu
        var = jnp.mean(zc * zc, axis=-1, keepdims=True)
        y = zc * lax.rsqrt(var + LN_EPS) * g_ref[...] + b_ref[...]
        o_ref[rs, :] = y
        obf_ref[rs, :] = y.astype(BF16)
        return carry

    lax.fori_loop(0, tr // ln_m, norm, 0, unroll=4)


def layer_norm_rows(z, gamma, beta, *, tr=512, ln_m=8):
    m, n = z.shape
    tr = min(tr, m)
    row = pl.BlockSpec((tr, n), lambda i: (i, 0))
    vec = pl.BlockSpec((1, n), lambda i: (0, 0))
    return pl.pallas_call(
        functools.partial(_ln_kernel, tr=tr, ln_m=ln_m), grid=(m // tr,),
        in_specs=[row, vec, vec], out_specs=[row, row],
        out_shape=[jax.ShapeDtypeStruct((m, n), F32), jax.ShapeDtypeStruct((m, n), BF16)],
        compiler_params=_cparams("parallel"), name="ln_rows",
    )(z, gamma, beta)


def _ep_resid(scale, accs, rows, tiles):
    return (ALPHA * tiles[0] + scale * accs[0],)


def ln_matmul(x, w, resid, gamma, beta, scale):
    z = fused_matmul([x], [w], functools.partial(_ep_resid, scale), [F32], resid.shape[1], tiles=[resid],
                     name="resid_mm")[0]
    return layer_norm_rows(z, gamma, beta)


def _lb_kernel(x_ref, lb_ref):
    x = x_ref[...]
    mx = jnp.max(x, axis=0, keepdims=True)
    e = jnp.exp(x - mx)
    p = e / jnp.sum(e, axis=0, keepdims=True)
    c = p[0:1, :]
    first = c
    lb_ref[0:1, :] = c - first
    for i in range(1, DEPTH):
        c = c + p[i:i + 1, :]
        lb_ref[i:i + 1, :] = c - first


def lower_bounds(logits):
    return pl.pallas_call(_lb_kernel, out_shape=jax.ShapeDtypeStruct(logits.shape, F32), name="hgrn_lb")(logits)


def _gla_levels(chunk):
    hs = []
    h = SUBLANES
    while h < chunk:
        hs.append(h)
        h *= 2
    return hs


def _gla_masks(chunk, dk):
    c = chunk
    t = np.arange(c)
    tril = (t[:, None] >= t[None, :]).astype(np.float32)
    pair, sgn = [], []
    for h in _gla_levels(c):
        is_q = (t % (2 * h)) >= h
        same = (t[:, None] // (2 * h)) == (t[None, :] // (2 * h))
        pair.append((same & is_q[:, None] & ~is_q[None, :]).astype(np.float32))
        sgn.append(np.broadcast_to(np.where(is_q, 1.0, -1.0)[:, None], (c, dk)).astype(np.float32))
    lane = np.arange(LANES)
    esel = np.tile(((lane[None, :] % SUBLANES) == np.arange(SUBLANES)[:, None]).astype(np.float32)[:, None, :],
                   (1, dk, 1)).reshape(SUBLANES * dk, LANES)
    diag = ((lane[None, :] < c) & (lane[None, :] // SUBLANES == t[:, None] // SUBLANES)
            & (t[:, None] % SUBLANES >= lane[None, :] % SUBLANES)).astype(np.float32)
    return (jnp.asarray(tril, BF16), jnp.asarray(np.stack(pair)), jnp.asarray(np.stack(sgn)),
            jnp.asarray(esel, BF16), jnp.asarray(diag))


def _gla_kernel(q_ref, k_ref, v_ref, g_ref, gate_ref, gamma_ref, tril_ref, pair_ref, sgn_ref, esel_ref, diag_ref,
                o_ref, st_ref, bs_ref, sc_ref, *, chunk, n_chunks, hb, dk, dv, q_scale, gate_act):
    c = chunk
    nblk = c // SUBLANES
    levels = _gla_levels(c)

    @pl.when(pl.program_id(1) == 0)
    def _():
        st_ref[...] = jnp.zeros(st_ref.shape, F32)

    heads = range(hb)
    kcs = [slice(hh * dk, (hh + 1) * dk) for hh in heads]
    vcs = [slice(hh * dv, (hh + 1) * dv) for hh in heads]

    def rows_of(ci):
        return pl.ds(ci * c if isinstance(ci, int) else pl.multiple_of(ci * c, c), c)

    def load_q(rs, hh):
        q = q_ref[rs, kcs[hh]]
        return q * q_scale if q_scale != 1.0 else q

    def front_prefix(ci, s):
        rs = rows_of(ci)
        tril = tril_ref[...]
        for hh in heads:
            bs_ref[s, hh, 0] = _prefix_rows(tril, g_ref[rs, kcs[hh]]) * LOG2E
            bs_ref[s, hh, 1] = k_ref[rs, kcs[hh]]

    def front_diag(ci, s):
        rs = rows_of(ci)
        scores = []
        for hh in heads:
            q = load_q(rs, hh)
            b2 = bs_ref[s, hh, 0]
            xs = []
            for j in range(SUBLANES):
                blocks = []
                for p in range(nblk):
                    r = p * SUBLANES + j
                    rows = slice(p * SUBLANES, (p + 1) * SUBLANES)
                    bj = bs_ref[s, hh, 0, r:r + 1, :]
                    kj = bs_ref[s, hh, 1, r:r + 1, :]
                    blocks.append(q[rows] * kj * jnp.exp2(jnp.minimum(b2[rows] - bj, 0.0)))
                xs.append(jnp.concatenate(blocks, axis=0).astype(BF16))
            scores.append((_dot(jnp.concatenate(xs, axis=1), esel_ref[...]) * diag_ref[...])[:, :c])
        return scores

    def front_tree(ci, s, scores):
        rs = rows_of(ci)
        for li, h in enumerate(levels):
            for hh in heads:
                q = load_q(rs, hh)
                k = bs_ref[s, hh, 1]
                b2 = bs_ref[s, hh, 0]
                ref = jnp.concatenate(
                    [jnp.broadcast_to(bs_ref[s, hh, 0, m * 2 * h + h - 1:m * 2 * h + h, :], (2 * h, dk))
                     for m in range(c // (2 * h))], axis=0)
                sg = sgn_ref[li]
                z = (jnp.where(sg > 0.0, q, k) * jnp.exp2(sg * (b2 - ref))).astype(BF16)
                scores[hh] = scores[hh] + _dot_nt(z, z) * pair_ref[li]
        for hh in heads:
            sc_ref[s, hh] = scores[hh].astype(BF16)

    def back_state(ci, s):
        rs = rows_of(ci)
        return [_dot_nt((load_q(rs, hh) * jnp.exp2(bs_ref[s, hh, 0])).astype(BF16), st_ref[hh].astype(BF16))
                for hh in heads]

    def back_out(ci, s, o_inter):
        rs = rows_of(ci)
        return [o_inter[hh] + _dot(sc_ref[s, hh], v_ref[rs, vcs[hh]].astype(BF16)) for hh in heads]

    def back_update(ci, s, outs):
        rs = rows_of(ci)
        for hh in heads:
            vb = v_ref[rs, vcs[hh]].astype(BF16)
            b_last = bs_ref[s, hh, 0, c - 1:c, :]
            kd = (bs_ref[s, hh, 1] * jnp.exp2(b_last - bs_ref[s, hh, 0])).astype(BF16)
            st_ref[hh] = st_ref[hh] * jnp.exp2(b_last) + _dot_tn(vb, kd)
        for hh in heads:
            o = outs[hh]
            ms = jnp.mean(o * o, axis=-1, keepdims=True)
            y = o * lax.rsqrt(ms + RMS_EPS) * gamma_ref[:, vcs[hh]] * gate_act(gate_ref[rs, vcs[hh]])
            o_ref[rs, vcs[hh]] = y.astype(o_ref.dtype)

    def overlapped(cb, sb, cf, sf):
        o_inter = back_state(cb, sb)
        front_prefix(cf, sf)
        outs = back_out(cb, sb, o_inter)
        scores = front_diag(cf, sf)
        back_update(cb, sb, outs)
        front_tree(cf, sf, scores)

    front_prefix(0, 0)
    front_tree(0, 0, front_diag(0, 0))

    def step(i, carry):
        overlapped(2 * i, 0, 2 * i + 1, 1)
        overlapped(2 * i + 1, 1, jnp.minimum(2 * i + 2, n_chunks - 1), 0)
        return carry

    lax.fori_loop(0, n_chunks // 2, step, 0)


def gla_recurrence(q, k, v, g, gate, gamma, *, heads, hb, dk, dv, gate_act, q_scale=1.0, offs=(0, 0, 0, 0, 0),
                   tb=1024, chunk=GLA_CHUNK, name="gla"):
    t = q.shape[0]
    tb = min(tb, t)
    masks = _gla_masks(chunk, dk)
    body = functools.partial(_gla_kernel, chunk=chunk, n_chunks=tb // chunk, hb=hb, dk=dk, dv=dv,
                             q_scale=q_scale, gate_act=gate_act)
    kspec = lambda off: pl.BlockSpec((tb, hb * dk), lambda h, i: (i, h + off))
    vspec = lambda off: pl.BlockSpec((tb, hb * dv), lambda h, i: (i, h + off))
    full = lambda a: pl.BlockSpec(a.shape, lambda h, i: (0,) * a.ndim)
    return pl.pallas_call(
        body, grid=(heads // hb, t // tb),
        in_specs=[kspec(offs[0]), kspec(offs[1]), vspec(offs[2]), kspec(offs[3]), vspec(offs[4]),
                  pl.BlockSpec((1, hb * dv), lambda h, i: (0, h))] + [full(a) for a in masks],
        out_specs=vspec(0),
        out_shape=jax.ShapeDtypeStruct((t, heads * dv), BF16),
        scratch_shapes=[pltpu.VMEM((hb, dv, dk), F32), pltpu.VMEM((2, hb, 2, chunk, dk), F32),
                        pltpu.VMEM((2, hb, chunk, chunk), BF16)],
        compiler_params=_cparams("parallel", "arbitrary"), name=name,
    )(q, k, v, g, gate, gamma, *masks)


SSD_HP = SSD_PAIR * SSM_HPG
SSD_ROWS = SUBLANES


def _ssd_bcast_matrix():
    sel = np.zeros((6 * SSD_ROWS, 2 * SSD_HP * LANES), np.float32)
    for h in range(SSD_HP):
        for part in range(3):
            sel[part * SSD_ROWS + h, h * LANES:(h + 1) * LANES] = 1.0
        for part in range(2):
            sel[(3 + part) * SSD_ROWS + h, (SSD_HP + h) * LANES:(SSD_HP + h + 1) * LANES] = 1.0
    return jnp.asarray(sel, BF16)


def _ssd_kernel(x_ref, b_ref, c_ref, z_ref, dtt_ref, biast_ref, at_ref, d_ref, gamma_ref, sel_ref,
                o_ref, st_ref, rowq_ref, colb_ref, yi_ref, *, n_chunks):
    c = SSD_CHUNK
    p = SSM_HEADDIM
    n = SSM_STATE
    npr = SSD_HP // 2

    @pl.when(pl.program_id(1) == 0)
    def _():
        st_ref[...] = jnp.zeros(st_ref.shape, F32)

    r2 = lax.broadcasted_iota(jnp.int32, (c, c), 0)
    c2 = lax.broadcasted_iota(jnp.int32, (c, c), 1)
    causal = r2 >= c2
    low = c2 < p
    top = r2 < p
    triu = (r2 <= c2).astype(BF16)
    biast = biast_ref[...]
    a_col = at_ref[...] * LOG2E

    tb = n_chunks * c
    dtt_st = jnp.concatenate([_softplus(dtt_ref[ci] + biast) for ci in range(n_chunks)], axis=0)
    e1, e2, e3 = _split3(dtt_st * jnp.concatenate([a_col] * n_chunks, axis=0))
    pre = _dot(jnp.concatenate([e1, e2, e3], axis=0), triu)
    nr = n_chunks * SSD_ROWS
    cst_st = (pre[:nr] + pre[nr:2 * nr]) + pre[2 * nr:]
    unstack = lambda v_: jnp.concatenate([v_[ci * SSD_ROWS:(ci + 1) * SSD_ROWS] for ci in range(n_chunks)], axis=1)
    dtt_all = unstack(dtt_st)
    cst_all = unstack(cst_st)
    c1, c2_, c3 = _split3(cst_all)
    d1 = dtt_all.astype(BF16)
    d2 = (dtt_all - d1.astype(F32)).astype(BF16)
    parts = jnp.concatenate([v_.astype(F32) for v_ in (c1, c2_, c3, d1, d2)]
                            + [jnp.zeros((SSD_ROWS, tb), F32)], axis=0).astype(BF16)
    colb_all = _dot_tn(parts, sel_ref[...])
    for ci in range(n_chunks):
        rowq_ref[ci, 0] = dtt_all[:, ci * c:(ci + 1) * c]
        rowq_ref[ci, 1] = cst_all[:, ci * c:(ci + 1) * c]
        colb_ref[ci] = colb_all[ci * c:(ci + 1) * c]

    def rows_of(ci):
        return pl.ds(ci * c if isinstance(ci, int) else pl.multiple_of(ci * c, c), c)

    def pair_ids(pr):
        h0, h1 = 2 * pr, 2 * pr + 1
        return slice(pr * LANES, (pr + 1) * LANES), h0, h1, h0 // SSM_HPG, h1 // SSM_HPG

    def group_mats(ref, rs):
        return [ref[rs, gl * n:(gl + 1) * n].astype(BF16) for gl in range(SSD_PAIR)]

    def front(ci, s):
        rs = rows_of(ci)
        dtt = rowq_ref[ci, 0]
        cst = rowq_ref[ci, 1]
        colb = colb_ref.at[ci]
        bis = group_mats(b_ref, rs)
        cms = group_mats(c_ref, rs)
        cbs = [_dot_nt(cms[gl], bis[gl]) for gl in range(SSD_PAIR)]
        for pr in range(npr):
            lanes, h0, h1, _, _ = pair_ids(pr)
            xp = x_ref[rs, lanes]
            xpb = xp.astype(BF16)
            ysel = []
            for h in (h0, h1):
                cs_b = colb[:, h * LANES:(h + 1) * LANES]
                lmat = jnp.where(causal, jnp.exp2(jnp.minimum(cs_b - cst[h:h + 1, :], 0.0)), 0.0)
                m = (cbs[h // SSM_HPG] * lmat * dtt[h:h + 1, :]).astype(BF16)
                ysel.append(_dot(m, xpb))
            yi_ref[s, pr] = jnp.where(low, ysel[0], ysel[1]) + d_ref[:, lanes] * xp

    def back_state(ci):
        cms = group_mats(c_ref, rows_of(ci))
        ystates = []
        for pr in range(npr):
            _, _, _, g0, g1 = pair_ids(pr)
            stb = st_ref[pr].astype(BF16)
            ystate = _dot_nt(cms[g0], stb)
            if g1 != g0:
                ystate = jnp.where(low, ystate, _dot_nt(cms[g1], stb))
            ystates.append(ystate)
        return ystates

    def back_finish(ci, s, ystates):
        rs = rows_of(ci)
        cst = rowq_ref[ci, 1]
        colb = colb_ref.at[ci]
        bis = group_mats(b_ref, rs)
        ypairs = []
        for pr in range(npr):
            lanes, h0, h1, g0, g1 = pair_ids(pr)
            xp = x_ref[rs, lanes]
            csb = jnp.where(low, colb[:, h0 * LANES:(h0 + 1) * LANES], colb[:, h1 * LANES:(h1 + 1) * LANES])
            dtb = jnp.where(low, colb[:, (SSD_HP + h0) * LANES:(SSD_HP + h0 + 1) * LANES],
                            colb[:, (SSD_HP + h1) * LANES:(SSD_HP + h1 + 1) * LANES])
            y = yi_ref[s, pr] + ystates[pr] * jnp.exp2(csb)
            last = csb[c - 1:c, :]
            xu = (xp * (dtb * jnp.exp2(last - csb))).astype(BF16)
            upd = _dot_tn(xu, bis[g0])
            if g1 != g0:
                upd = jnp.where(top, upd, _dot_tn(xu, bis[g1]))
            dec = jnp.where(top, jnp.exp2(cst[h0:h0 + 1, c - 1:c]), jnp.exp2(cst[h1:h1 + 1, c - 1:c]))
            st_ref[pr] = st_ref[pr] * dec + upd
            ypairs.append(y * _silu(z_ref[rs, lanes]))

        sq = [y * y for y in ypairs]
        zero = jnp.zeros((c, c), F32)
        ss0 = jnp.sum(sq[0] + jnp.where(low, sq[1], zero), axis=-1, keepdims=True)
        ss1 = jnp.sum(sq[2] + jnp.where(low, zero, sq[1]), axis=-1, keepdims=True)
        scale = 1.0 / (SSM_HPG * p)
        r0 = lax.rsqrt(ss0 * scale + RMS_EPS)
        r1 = lax.rsqrt(ss1 * scale + RMS_EPS)
        rinv = [r0, jnp.where(low, r0, r1), r1]
        for pr in range(npr):
            lanes = slice(pr * LANES, (pr + 1) * LANES)
            o_ref[rs, lanes] = (ypairs[pr] * rinv[pr] * gamma_ref[:, lanes]).astype(o_ref.dtype)

    def overlapped(cb, sb, cf, sf):
        ystates = back_state(cb)
        front(cf, sf)
        back_finish(cb, sb, ystates)

    front(0, 0)

    def step(i, carry):
        overlapped(2 * i, 0, 2 * i + 1, 1)
        overlapped(2 * i + 1, 1, jnp.minimum(2 * i + 2, n_chunks - 1), 0)
        return carry

    lax.fori_loop(0, n_chunks // 2, step, 0)


def ssd_recurrence(xbc, z, z_off, dt_raw, dt_bias, a_log, d_skip, gamma, *, tb=512):
    assert SSD_PAIR == 2 and SSM_HPG == 3 and 2 * SSM_HEADDIM == LANES and SSD_CHUNK == LANES
    t = xbc.shape[0]
    tb = min(tb, t)
    c = SSD_CHUNK
    npair = SSM_GROUPS // SSD_PAIR
    hp = SSD_HP
    wx = hp * SSM_HEADDIM
    wn = SSD_PAIR * SSM_STATE
    padr = lambda v_: jnp.pad(v_, [(0, 0)] * (v_.ndim - 2) + [(0, SSD_ROWS - hp), (0, 0)])
    dtt_p = padr(dt_raw.reshape(t // c, c, npair, hp).transpose(2, 0, 3, 1))
    a = -jnp.exp(a_log.astype(F32))
    colp = lambda v_: padr(v_.reshape(npair, hp, 1))
    d_lanes = jnp.repeat(d_skip.astype(F32), SSM_HEADDIM).reshape(1, SSM_WIDTH)
    sel = _ssd_bcast_matrix()
    body = functools.partial(_ssd_kernel, n_chunks=tb // c)
    b_off = SSM_WIDTH // wn
    c_off = (SSM_WIDTH + SSM_GROUPS * SSM_STATE) // wn
    pspec_col = pl.BlockSpec((None, SSD_ROWS, 1), lambda g, i: (g, 0, 0))
    lane_row = pl.BlockSpec((1, wx), lambda g, i: (0, g))
    return pl.pallas_call(
        body, grid=(npair, t // tb),
        in_specs=[pl.BlockSpec((tb, wx), lambda g, i: (i, g)),
                  pl.BlockSpec((tb, wn), lambda g, i: (i, g + b_off)),
                  pl.BlockSpec((tb, wn), lambda g, i: (i, g + c_off)),
                  pl.BlockSpec((tb, wx), lambda g, i: (i, g + z_off)),
                  pl.BlockSpec((None, tb // c, SSD_ROWS, c), lambda g, i: (g, i, 0, 0)),
                  pspec_col, pspec_col, lane_row, lane_row,
                  pl.BlockSpec(sel.shape, lambda g, i: (0, 0))],
        out_specs=pl.BlockSpec((tb, wx), lambda g, i: (i, g)),
        out_shape=jax.ShapeDtypeStruct((t, SSM_WIDTH), BF16),
        scratch_shapes=[pltpu.VMEM((hp // 2, 2 * SSM_HEADDIM, SSM_STATE), F32),
                        pltpu.VMEM((tb // c, 2, SSD_ROWS, c), F32),
                        pltpu.VMEM((tb // c, c, 2 * hp * LANES), F32),
                        pltpu.VMEM((2, hp // 2, c, LANES), F32)],
        compiler_params=_cparams("parallel", "arbitrary"), name="ssd",
    )(xbc, xbc, xbc, z, dtt_p, colp(dt_bias), colp(a), d_lanes, gamma.reshape(1, SSM_WIDTH), sel)


def _ep_swiglu(accs, rows, tiles):
    return (_silu(accs[0]) * accs[1],)


def _ep_identity(accs, rows, tiles):
    return (accs[0],)


def _ep_scaled_silu(scale, accs, rows, tiles):
    return (_silu(accs[0]) * scale,)


def _ep_hgrn_forget(accs, rows, tiles):
    zf = accs[0]
    lb = rows[0]
    la = jnp.log(lb)
    lc = jnp.log1p(-lb) + _log_sigmoid(zf)
    log_f = jnp.maximum(la, lc) + jnp.log1p(jnp.exp(-jnp.abs(la - lc)))
    return (log_f, (1.0 - lb) * _sigmoid(-zf))


def _ep_log_alpha(accs, rows, tiles):
    return (_log_sigmoid(accs[0] + rows[0]) * (1.0 / GLA_TAU),)


def _ep_merge(accs, rows, tiles):
    return (_sigmoid(accs[0]) * accs[3] + _sigmoid(accs[1]) * accs[4] + _sigmoid(accs[2]) * accs[5],)


def _ep_ple(accs, rows, tiles):
    y = tiles[0] + _sigmoid(accs[0]) * accs[1]
    return (y, y)


def _regroup_w_in(w_in):
    o = IN_OFFSETS
    w = w_in.astype(BF16)
    small = jnp.concatenate([w[..., o[6]:o[7]], w[..., o[11]:o[12]]], axis=-1)
    small = jnp.pad(small, ((0, 0), (0, 0), (0, LANES - small.shape[-1])))
    return w, w[..., o[7]:o[11]], w[..., o[12]:], small


def _ffn(h, hb, w_in, w_out, lead, g, b):
    mid = fused_matmul([hb], [W(w_in, 0, 0, lead), W(w_in, 0, D_FF // 512, lead)], _ep_swiglu, [BF16], D_FF,
                       name="ffn_in")[0]
    return ln_matmul(mid, W(w_out, 0, 0, lead), h, g[None, :], b[None, :], 0.5)


def _mixer(hb, layer, lb, w_all, conv_w, conv_b, dt_bias, a_log, d_skip, gate_w, gla_b_gate, mix_norm_g, w_branch):
    lead = (layer,)
    tn = 512
    w_a, w_b, w_c, w_s = w_all
    wcol = lambda col: W(w_a, 0, col // tn, lead)
    g_hg, g_ssm, g_gla = mix_norm_g[:HG_WIDTH], mix_norm_g[HG_WIDTH:HG_WIDTH + SSM_WIDTH], \
        mix_norm_g[HG_WIDTH + SSM_WIDTH:]

    hg_q = fused_matmul([hb], [wcol(COL_HG_Q)], functools.partial(_ep_scaled_silu, HG_DK ** -0.5), [F32],
                        HG_WIDTH, name="proj_hgq")[0]
    hg_logf, hg_k = fused_matmul([hb], [wcol(COL_HG_F)], _ep_hgrn_forget, [F32, F32], HG_WIDTH,
                                 rows=[lb[None, :]], name="proj_hgf")
    id1 = fused_matmul([hb], [wcol(COL_ID1)], _ep_identity, [F32], ID1_WIDTH, name="proj_id1")[0]
    xbc = conv_matmul(hb, wcol(COL_XBC), conv_w, conv_b[None, :], SSM_XBC)
    small = fused_matmul([hb], [W(w_s, 0, 0, lead)], _ep_identity, [F32], LANES, tn=LANES, name="proj_small")[0]
    id2 = fused_matmul([hb], [W(w_b, 0, 0, lead)], _ep_identity, [F32], ID2_WIDTH, name="proj_id2")[0]
    log_alpha = fused_matmul([small], [W(gate_w, 0, 0, lead)], _ep_log_alpha, [F32], GLA_HEADS * GLA_DK,
                             rows=[gla_b_gate[None, :]], name="gla_gate")[0]

    hb_hg = 6
    y_hg = gla_recurrence(hg_q, hg_k, id1, hg_logf, id1, g_hg[None, :], heads=HG_HEADS, hb=hb_hg, dk=HG_DK,
                          dv=HG_DK, gate_act=_sigmoid, offs=(0, 0, 0, 0, HG_WIDTH // (hb_hg * HG_DK)),
                          name="hgrn")
    y_ssm = ssd_recurrence(xbc, id1, 2 * HG_WIDTH // (SSD_PAIR * SSM_HPG * SSM_HEADDIM), small[:, :SSM_HEADS],
                           dt_bias, a_log, d_skip, g_ssm)
    hb_gla = 4
    wk, wv = hb_gla * GLA_DK, hb_gla * GLA_DV
    y_gla = gla_recurrence(id2, id2, id2, log_alpha, id2, g_gla[None, :], heads=GLA_HEADS, hb=hb_gla, dk=GLA_DK,
                           dv=GLA_DV, gate_act=_silu, q_scale=GLA_DK ** -0.5,
                           offs=(0, GLA_HEADS * GLA_DK // wk, 2 * GLA_HEADS * GLA_DK // wv, 0,
                                 (2 * GLA_HEADS * GLA_DK + GLA_WIDTH) // wv), name="gla")

    tn_m = 256
    nb = D_MODEL // tn_m
    merged = fused_matmul(
        [hb, y_hg, y_ssm, y_gla],
        [W(w_c, 0, 0, lead), W(w_c, 0, nb, lead), W(w_c, 0, 2 * nb, lead),
         W(w_branch, 1, 0, lead, HG_WIDTH, 0), W(w_branch, 2, 0, lead, SSM_WIDTH, 1),
         W(w_branch, 3, 0, lead, GLA_WIDTH, (HG_WIDTH + SSM_WIDTH) // GLA_WIDTH)],
        _ep_merge, [BF16], D_MODEL, tm=512, tn=tn_m, name="merge")[0]
    return merged


def kernel(x, p, ffn_w_in, ffn_w_out, ln_g, ln_b, w_in, hg_lb_logits, ssm_conv_w, ssm_conv_b, ssm_dt_bias,
           ssm_a_log, ssm_d, gla_w_gate, gla_b_gate, mix_norm_g, w_branch, w_out, ple_w_proj, ple_w_gate):
    bsz, t, d = x.shape
    lb_all = lower_bounds(hg_lb_logits.astype(F32))
    ffn_w_in_b = ffn_w_in.astype(BF16)
    ffn_w_out_b = ffn_w_out.astype(BF16)
    w_all = _regroup_w_in(w_in)
    w_branch_b = w_branch.astype(BF16)
    w_out_b = w_out.astype(BF16)
    ple_w_gate_b = ple_w_gate.astype(BF16)
    ple_w_proj_b = ple_w_proj.astype(BF16)
    gate_w = jnp.zeros((DEPTH, LANES, GLA_HEADS * GLA_DK), F32).at[:, SSM_HEADS:SSM_HEADS + GLA_RANK].set(
        gla_w_gate).astype(BF16)
    outs = []
    for bi in range(bsz):
        h = x[bi]
        hb = h.astype(BF16)
        for i in range(DEPTH):
            h, hb = _ffn(h, hb, ffn_w_in_b, ffn_w_out_b, (i, 0), ln_g[i, 0], ln_b[i, 0])
            merged = _mixer(hb, i, lb_all[i], w_all, ssm_conv_w[i], ssm_conv_b[i], ssm_dt_bias[i], ssm_a_log[i],
                            ssm_d[i], gate_w, gla_b_gate[i], mix_norm_g[i], w_branch_b)
            h, hb = ln_matmul(merged, W(w_out_b, 0, 0, (i,)), h, ln_g[i, 1][None, :], ln_b[i, 1][None, :], 1.0)
            h, hb = _ffn(h, hb, ffn_w_in_b, ffn_w_out_b, (i, 1), ln_g[i, 2], ln_b[i, 2])
            h, hb = fused_matmul([hb, p[i, bi]], [W(ple_w_gate_b, 0, 0, (i,)), W(ple_w_proj_b, 1, 0, (i,))],
                                 _ep_ple, [F32, BF16], D_MODEL, tiles=[h], name="ple")
        outs.append(h)
    return jnp.stack(outs, axis=0)
```
